```python
import jax
import jax.numpy as jnp
from jax import lax
import numpy as np

D_MODEL = 1024
BATCH = 16
SEQ = 2048
DEPTH = 4

HEAD_DIM = 64
BRANCH_W = 384
N_BRANCHES = 3
RWKV_HEADS = BRANCH_W // HEAD_DIM
RWKV_W = RWKV_HEADS * HEAD_DIM
DECAY_LORA = 64
AAA_LORA = 64
GATE_LORA = 160
RWKV_GN_EPS = 64e-5
SB_HEADS = BRANCH_W // HEAD_DIM
SB_W = SB_HEADS * HEAD_DIM
MLA_HEADS = 6
MLA_Q_LORA = 256
MLA_KV_LORA = 128
MLA_NOPE_DIM = 64
MLA_ROPE_DIM = 32
MLA_V_DIM = 64
MLA_W = MLA_HEADS * MLA_V_DIM
ROPE_THETA = 10000.0
MAX_POS_OFFSET = 1024
D_FF = 2816
MACARON_WEIGHT = 0.5
Q_BLOCK = 128
NORM_EPS = 1e-6
N_ADA = 9
RWKV_COLS = 3 * RWKV_W + DECAY_LORA + AAA_LORA + GATE_LORA
REST_SPLITS = (SB_W, SB_W, SB_W, MLA_Q_LORA, MLA_KV_LORA, MLA_ROPE_DIM, N_BRANCHES * D_MODEL)
D_IN = RWKV_COLS + 3 * SB_W + MLA_Q_LORA + MLA_KV_LORA + MLA_ROPE_DIM + N_BRANCHES * D_MODEL

kernel_name = 'hybrid_rwkv7_stickbreak_mla_macaron_adaln'


def _rms_norm(x, g):
    xf = x.astype(jnp.float32)
    y = xf * lax.rsqrt(jnp.mean(xf * xf, axis=-1, keepdims=True) + NORM_EPS)
    return (y * g.astype(jnp.float32)).astype(x.dtype)


def _modulated_norm(x, g, shift, scale):
    return _rms_norm(x, g) * (1.0 + scale[:, None, :]) + shift[:, None, :]


def _swiglu(h, w_in, w_out):
    gate, up = jnp.split(h @ w_in, 2, axis=-1)
    return (jax.nn.silu(gate) * up) @ w_out


def _split(p, sizes):
    out, off = [], 0
    for s in sizes:
        out.append(p[..., off:off + s])
        off += s
    return out


def _heads(t, n_heads):
    b, s, _ = t.shape
    return t.reshape(b, s, n_heads, -1)


def _token_shift(p, mu):
    prev = jnp.pad(p, ((0, 0), (1, 0), (0, 0)))[:, :-1]
    return p + (prev - p) * mu


def _wkv7_scan(r, w, k, v, a, b):
    bsz, _, n_heads, n = r.shape

    def step(state, inp):
        r_t, w_t, k_t, v_t, a_t, b_t = inp
        sa = jnp.einsum('bhvk,bhk->bhv', state, a_t)
        state = (state * w_t[:, :, None, :] + sa[..., None] * b_t[:, :, None, :]
                 + v_t[..., None] * k_t[:, :, None, :])
        return state, jnp.einsum('bhvk,bhk->bhv', state, r_t)

    xs = tuple(jnp.swapaxes(t, 0, 1) for t in (r, w, k, v, a, b))
    state0 = jnp.zeros((bsz, n_heads, n, n), jnp.float32)
    _, ys = lax.scan(step, state0, xs)
    return jnp.swapaxes(ys, 0, 1)


def _rwkv7_branch(r, k, v, dw, da, dg, w0, w2, a0, a2, g2, k_k, k_a, r_k, lnx_g, lnx_b):
    out_dtype = r.dtype
    f32 = jnp.float32
    r, k, v, dw, da, dg = (t.astype(f32) for t in (r, k, v, dw, da, dg))
    bsz, seq, _ = r.shape
    w_log = -jax.nn.softplus(-(w0 + jnp.tanh(dw) @ w2)) - 0.5
    decay = jnp.exp(-jnp.exp(w_log))
    a = jax.nn.sigmoid(a0 + da @ a2)
    g = jax.nn.sigmoid(dg) @ g2
    kk = _heads(k * k_k, RWKV_HEADS)
    kk = kk * lax.rsqrt(jnp.maximum(jnp.sum(kk * kk, axis=-1, keepdims=True), 1e-24))
    k = k * (1.0 + (a - 1.0) * k_a)
    rh, kh, vh, ah, wh = (_heads(t, RWKV_HEADS) for t in (r, k, v, a, decay))
    y = _wkv7_scan(rh, wh, kh, vh, -kk, kk * ah)
    mean = jnp.mean(y, axis=-1, keepdims=True)
    var = jnp.mean(jnp.square(y - mean), axis=-1, keepdims=True)
    y = ((y - mean) * lax.rsqrt(var + RWKV_GN_EPS)).reshape(bsz, seq, RWKV_W) * lnx_g + lnx_b
    bonus = jnp.sum(rh * kh * r_k, axis=-1, keepdims=True) * vh
    y = (y + bonus.reshape(bsz, seq, RWKV_W)) * g
    return y.astype(out_dtype)


def _stick_breaking_branch(q, k, v):
    bsz, seq, _ = q.shape
    q, k, v = (_heads(t, SB_HEADS) for t in (q, k, v))
    scale = HEAD_DIM ** -0.5
    outs = []
    for start in range(0, seq, Q_BLOCK):
        end = min(start + Q_BLOCK, seq)
        kp, vp = k[:, :end], v[:, :end]
        z = jnp.einsum('bqhd,bkhd->bhqk', q[:, start:end], kp).astype(jnp.float32) * scale
        t_idx = jnp.arange(start, end)
        s_idx = jnp.arange(end)
        mask = s_idx[None, :] < t_idx[:, None]
        log_1m = jnp.where(mask, jax.nn.log_sigmoid(-z), 0.0)
        rev = lax.cumsum(log_1m, axis=3, reverse=True)
        after = jnp.concatenate([rev[..., 1:], jnp.zeros_like(rev[..., :1])], axis=-1)
        weights = jnp.where(mask, jnp.exp(jax.nn.log_sigmoid(z) + after), 0.0)
        outs.append(jnp.einsum('bhqk,bkhd->bqhd', weights.astype(vp.dtype), vp))
    return jnp.concatenate(outs, axis=1).reshape(bsz, seq, SB_W)


def _rope_cos_sin(positions):
    inv_freq = 1.0 / (ROPE_THETA ** (jnp.arange(0, MLA_ROPE_DIM, 2, dtype=jnp.float32) / MLA_ROPE_DIM))
    ang = positions.astype(jnp.float32)[..., None] * inv_freq
    return jnp.cos(ang)[:, :, None, :], jnp.sin(ang)[:, :, None, :]


def _apply_rope(x, cos, sin):
    xf = x.astype(jnp.float32)
    half = MLA_ROPE_DIM // 2
    x1, x2 = xf[..., :half], xf[..., half:]
    return jnp.concatenate([x1 * cos - x2 * sin, x2 * cos + x1 * sin], axis=-1).astype(x.dtype)


def _mla_branch(cq, ckv, krope, positions, q_norm_g, w_uq, kv_norm_g, w_ukv):
    bsz, seq, _ = cq.shape
    q = (_rms_norm(cq, q_norm_g) @ w_uq).reshape(bsz, seq, MLA_HEADS, MLA_NOPE_DIM + MLA_ROPE_DIM)
    kv = (_rms_norm(ckv, kv_norm_g) @ w_ukv).reshape(bsz, seq, MLA_HEADS, MLA_NOPE_DIM + MLA_V_DIM)
    cos, sin = _rope_cos_sin(positions)
    q_nope = q[..., :MLA_NOPE_DIM]
    q_rope = _apply_rope(q[..., MLA_NOPE_DIM:], cos, sin)
    k_nope, v = kv[..., :MLA_NOPE_DIM], kv[..., MLA_NOPE_DIM:]
    k_rope = _apply_rope(krope[:, :, None, :], cos, sin)[:, :, 0]
    scale = (MLA_NOPE_DIM + MLA_ROPE_DIM) ** -0.5
    outs = []
    for start in range(0, seq, Q_BLOCK):
        end = min(start + Q_BLOCK, seq)
        s = (jnp.einsum('bqhd,bkhd->bhqk', q_nope[:, start:end], k_nope[:, :end])
             + jnp.einsum('bqhr,bkr->bhqk', q_rope[:, start:end], k_rope[:, :end])).astype(jnp.float32) * scale
        mask = jnp.arange(end)[None, :] <= jnp.arange(start, end)[:, None]
        probs = jax.nn.softmax(jnp.where(mask, s, -jnp.inf), axis=-1)
        outs.append(jnp.einsum('bhqk,bkhd->bqhd', probs.astype(v.dtype), v[:, :end]))
    return jnp.concatenate(outs, axis=1).reshape(bsz, seq, MLA_W)


def _token_mixer(h, positions, w_in, mu, w0, w2, a0, a2, g2, k_k, k_a, r_k, lnx_g, lnx_b,
                 q_norm_g, w_uq, kv_norm_g, w_ukv, branch_w, w_out):
    bsz, seq, _ = h.shape
    p = h @ w_in
    p_rwkv = _token_shift(p[..., :RWKV_COLS], mu)
    r, k, v, dw, da, dg = _split(p_rwkv, (RWKV_W, RWKV_W, RWKV_W, DECAY_LORA, AAA_LORA, GATE_LORA))
    sb_q, sb_k, sb_v, cq, ckv, krope, gate_logits = _split(p[..., RWKV_COLS:], REST_SPLITS)
    y_a = _rwkv7_branch(r, k, v, dw, da, dg, w0, w2, a0, a2, g2, k_k, k_a, r_k, lnx_g, lnx_b)
    y_b = _stick_breaking_branch(sb_q, sb_k, sb_v)
    y_c = _mla_branch(cq, ckv, krope, positions, q_norm_g, w_uq, kv_norm_g, w_ukv)
    ys = jnp.stack([y_a, y_b, y_c], axis=2)
    proj = jnp.einsum('bsnw,nwd->bsnd', ys, branch_w)
    gates = jax.nn.sigmoid(gate_logits.reshape(bsz, seq, N_BRANCHES, D_MODEL))
    merged = jnp.sum(gates * proj, axis=2)
    return merged @ w_out


def setup_inputs(seed: int = 0) -> dict:
    key = jax.random.key(seed)
    ks = jax.random.split(key, 32)
    f32 = jnp.float32

    def nrm(k, shape, fan_in, gain=1.0):
        return jax.random.normal(k, shape, f32) * (gain * fan_in ** -0.5)

    def noisy(k, shape, center, std):
        return center + std * jax.random.normal(k, shape, f32)

    x = jax.random.normal(ks[0], (BATCH, SEQ, D_MODEL), f32)
    c = jax.random.normal(ks[1], (BATCH, D_MODEL), f32)
    positions = (jnp.arange(SEQ, dtype=jnp.int32)[None, :]
                 + jax.random.randint(ks[2], (BATCH, 1), 0, MAX_POS_OFFSET, dtype=jnp.int32))
    return {
        'x': x,
        'c': c,
        'positions': positions,
        'ada_w': nrm(ks[3], (DEPTH, D_MODEL, N_ADA * D_MODEL), D_MODEL, 0.5),
        'ada_b': noisy(ks[4], (DEPTH, N_ADA * D_MODEL), 0.0, 0.01),
        'norm_g': noisy(ks[5], (DEPTH, 3, D_MODEL), 1.0, 0.05),
        'ffn1_w_in': nrm(ks[6], (DEPTH, D_MODEL, 2 * D_FF), D_MODEL),
        'ffn1_w_out': nrm(ks[7], (DEPTH, D_FF, D_MODEL), D_FF),
        'mix_w_in': nrm(ks[8], (DEPTH, D_MODEL, D_IN), D_MODEL),
        'rwkv_mu': jax.random.uniform(ks[9], (DEPTH, RWKV_COLS), f32),
        'rwkv_w0': jax.random.uniform(ks[10], (DEPTH, RWKV_W), f32, -6.0, 1.0),
        'rwkv_w2': nrm(ks[11], (DEPTH, DECAY_LORA, RWKV_W), DECAY_LORA),
        'rwkv_a0': noisy(ks[12], (DEPTH, RWKV_W), 0.0, 0.5),
        'rwkv_a2': nrm(ks[13], (DEPTH, AAA_LORA, RWKV_W), AAA_LORA),
        'rwkv_g2': nrm(ks[14], (DEPTH, GATE_LORA, RWKV_W), GATE_LORA),
        'rwkv_k_k': noisy(ks[15], (DEPTH, RWKV_W), 0.85, 0.05),
        'rwkv_k_a': noisy(ks[16], (DEPTH, RWKV_W), 1.0, 0.05),
        'rwkv_r_k': noisy(ks[17], (DEPTH, RWKV_HEADS, HEAD_DIM), 0.0, 0.1),
        'rwkv_lnx_g': noisy(ks[18], (DEPTH, RWKV_W), 1.0, 0.05),
        'rwkv_lnx_b': noisy(ks[19], (DEPTH, RWKV_W), 0.0, 0.01),
        'mla_q_norm_g': noisy(ks[20], (DEPTH, MLA_Q_LORA), 1.0, 0.05),
        'mla_w_uq': nrm(ks[21], (DEPTH, MLA_Q_LORA, MLA_HEADS * (MLA_NOPE_DIM + MLA_ROPE_DIM)), MLA_Q_LORA),
        'mla_kv_norm_g': noisy(ks[22], (DEPTH, MLA_KV_LORA), 1.0, 0.05),
        'mla_w_ukv': nrm(ks[23], (DEPTH, MLA_KV_LORA, MLA_HEADS * (MLA_NOPE_DIM + MLA_V_DIM)), MLA_KV_LORA),
        'branch_w': nrm(ks[24], (DEPTH, N_BRANCHES, BRANCH_W, D_MODEL), BRANCH_W),
        'mix_w_out': nrm(ks[25], (DEPTH, D_MODEL, D_MODEL), D_MODEL),
        'ffn2_w_in': nrm(ks[26], (DEPTH, D_MODEL, 2 * D_FF), D_MODEL),
        'ffn2_w_out': nrm(ks[27], (DEPTH, D_FF, D_MODEL), D_FF),
        'final_norm_g': noisy(ks[28], (D_MODEL,), 1.0, 0.05),
    }


def reference(x, c, positions, ada_w, ada_b, norm_g, ffn1_w_in, ffn1_w_out, mix_w_in,
              rwkv_mu, rwkv_w0, rwkv_w2, rwkv_a0, rwkv_a2, rwkv_g2, rwkv_k_k, rwkv_k_a,
              rwkv_r_k, rwkv_lnx_g, rwkv_lnx_b, mla_q_norm_g, mla_w_uq, mla_kv_norm_g,
              mla_w_ukv, branch_w, mix_w_out, ffn2_w_in, ffn2_w_out, final_norm_g):
    bsz = x.shape[0]
    c_act = jax.nn.silu(c)
    for l in range(DEPTH):
        mod = (c_act @ ada_w[l] + ada_b[l]).reshape(bsz, 3, 3, D_MODEL)
        h = _modulated_norm(x, norm_g[l, 0], mod[:, 0, 0], mod[:, 0, 1])
        x = x + MACARON_WEIGHT * mod[:, 0, 2][:, None, :] * _swiglu(h, ffn1_w_in[l], ffn1_w_out[l])
        h = _modulated_norm(x, norm_g[l, 1], mod[:, 1, 0], mod[:, 1, 1])
        mix = _token_mixer(h, positions, mix_w_in[l], rwkv_mu[l], rwkv_w0[l], rwkv_w2[l],
                           rwkv_a0[l], rwkv_a2[l], rwkv_g2[l], rwkv_k_k[l], rwkv_k_a[l],
                           rwkv_r_k[l], rwkv_lnx_g[l], rwkv_lnx_b[l], mla_q_norm_g[l],
                           mla_w_uq[l], mla_kv_norm_g[l], mla_w_ukv[l], branch_w[l], mix_w_out[l])
        x = x + mod[:, 1, 2][:, None, :] * mix
        h = _modulated_norm(x, norm_g[l, 2], mod[:, 2, 0], mod[:, 2, 1])
        x = x + MACARON_WEIGHT * mod[:, 2, 2][:, None, :] * _swiglu(h, ffn2_w_in[l], ffn2_w_out[l])
    return _rms_norm(x, final_norm_g)
```

```python
import functools

import jax
import jax.numpy as jnp
from jax import lax
from jax.experimental import pallas as pl
from jax.experimental.pallas import tpu as pltpu

F32 = jnp.float32
BF16 = jnp.bfloat16

HEAD_DIM = 64
N_HEADS = 6
BRANCH_W = N_HEADS * HEAD_DIM
N_BRANCHES = 3
DECAY_LORA = 64
AAA_LORA = 64
GATE_LORA = 160
RWKV_GN_EPS = 64e-5
MLA_Q_LORA = 256
MLA_KV_LORA = 128
MLA_NOPE_DIM = 64
MLA_ROPE_DIM = 32
MLA_V_DIM = 64
ROPE_THETA = 10000.0
MACARON_WEIGHT = 0.5
NORM_EPS = 1e-6
N_ADA = 9
RWKV_COLS = 3 * BRANCH_W + DECAY_LORA + AAA_LORA + GATE_LORA

LANES = 128
SUBLANES = 8
VMEM_LIMIT_BYTES = 56 * 1024 * 1024

RWKV_PAD = 1536
GATE_LORA_PAD = RWKV_PAD - (3 * BRANCH_W + DECAY_LORA + AAA_LORA)
NEG_BIG = -1e30


def _tile(n, pref):
    t = min(n, pref)
    assert n % t == 0, (n, t)
    return t


def _params(sem):
    return pltpu.CompilerParams(dimension_semantics=sem, vmem_limit_bytes=VMEM_LIMIT_BYTES)


def _const_spec(shape):
    nd = len(shape)
    return pl.BlockSpec(shape, lambda *_: (0,) * nd)


def _mod_spec(l, j, d):
    return pl.BlockSpec((None, None, None, 1, d), lambda b, *_: (l, j, b, 0, 0))


def _modulated_norm(x, g, shift, scale):
    ms = jnp.mean(x * x, axis=-1, keepdims=True)
    y = x * lax.rsqrt(ms + NORM_EPS) * g
    return y * (1.0 + scale) + shift


def _rms(x, g):
    ms = jnp.mean(x * x, axis=-1, keepdims=True)
    return x * lax.rsqrt(ms + NORM_EPS) * g


def _dot(a, b):
    return jnp.dot(a, b, preferred_element_type=F32)


def _dot_nt(a, b):
    return lax.dot_general(a, b, (((1,), (1,)), ((), ())), preferred_element_type=F32)


def _ada_kernel(c_ref, w_ref, b_ref, o_ref):
    c = c_ref[...]
    ca = (c * jax.nn.sigmoid(c)).astype(BF16)
    o_ref[...] = _dot(ca, w_ref[...].astype(BF16)) + b_ref[...]


def _ada_call(c, ada_w, ada_b):
    depth, d, _ = ada_w.shape
    bsz = c.shape[0]
    ada_b4 = ada_b.reshape(depth, N_ADA, 1, d)
    out = pl.pallas_call(
        _ada_kernel,
        out_shape=jax.ShapeDtypeStruct((depth, N_ADA, bsz, d), F32),
        grid=(depth, N_ADA),
        in_specs=[
            pl.BlockSpec((bsz, d), lambda l, j: (0, 0)),
            pl.BlockSpec((None, d, d), lambda l, j: (l, 0, j)),
            pl.BlockSpec((None, None, 1, d), lambda l, j: (l, j, 0, 0)),
        ],
        out_specs=pl.BlockSpec((None, None, bsz, d), lambda l, j: (l, j, 0, 0)),
        compiler_params=_params(("parallel", "parallel")),
        name="ada_mod",
    )(c, ada_w, ada_b4)
    return out.reshape(depth, N_ADA, bsz, 1, d)


def _rope_kernel(pos_ref, freq_ref, cos_ref, sin_ref):
    ang = pos_ref[...].astype(F32) * freq_ref[...]
    lane = lax.broadcasted_iota(jnp.int32, ang.shape, 1)
    in_rope = (lane >= MLA_NOPE_DIM) & (lane < MLA_NOPE_DIM + MLA_ROPE_DIM)
    cos_ref[...] = jnp.where(lane < MLA_NOPE_DIM, 1.0, jnp.where(in_rope, jnp.cos(ang), 0.0))
    sin_ref[...] = jnp.where(in_rope, jnp.sin(ang), 0.0)


def _rope_call(positions):
    bsz, seq = positions.shape
    half = MLA_ROPE_DIM // 2
    inv_freq = 1.0 / (ROPE_THETA ** (jnp.arange(0, MLA_ROPE_DIM, 2, dtype=F32) / MLA_ROPE_DIM))
    freq = jnp.zeros((1, LANES), F32)
    freq = freq.at[0, MLA_NOPE_DIM:MLA_NOPE_DIM + half].set(inv_freq)
    freq = freq.at[0, MLA_NOPE_DIM + half:MLA_NOPE_DIM + 2 * half].set(inv_freq)
    ts = _tile(seq, 512)
    spec = pl.BlockSpec((None, ts, LANES), lambda b, i: (b, i, 0))
    return pl.pallas_call(
        _rope_kernel,
        out_shape=(jax.ShapeDtypeStruct((bsz, seq, LANES), F32),) * 2,
        grid=(bsz, seq // ts),
        in_specs=[pl.BlockSpec((None, ts, 1), lambda b, i: (b, i, 0)), _const_spec((1, LANES))],
        out_specs=(spec, spec),
        compiler_params=_params(("parallel", "parallel")),
        name="rope_tables",
    )(positions.reshape(bsz, seq, 1), freq)


def _ffn_kernel(x_ref, sh_ref, sc_ref, gt_ref, g_ref, wg_ref, wu_ref, wo_ref, *rest, final):
    if final:
        fg_ref, o_ref, h_ref, acc_ref = rest
    else:
        o_ref, h_ref, acc_ref = rest
    j = pl.program_id(2)

    @pl.when(j == 0)
    def _():
        h = _modulated_norm(x_ref[...], g_ref[...], sh_ref[...], sc_ref[...])
        h_ref[...] = h.astype(BF16)
        acc_ref[...] = jnp.zeros_like(acc_ref)

    h = h_ref[...]
    gate = _dot(h, wg_ref[...])
    up = _dot(h, wu_ref[...])
    act = (gate * jax.nn.sigmoid(gate) * up).astype(BF16)
    acc_ref[...] += _dot(act, wo_ref[...])

    @pl.when(j == pl.num_programs(2) - 1)
    def _():
        out = x_ref[...] + (MACARON_WEIGHT * gt_ref[...]) * acc_ref[...]
        if final:
            out = _rms(out, fg_ref[...])
        o_ref[...] = out


def _ffn_call(x, mod5, l, sub, norm_g, w_in, w_out, final_g=None):
    bsz, seq, d = x.shape
    d_ff = w_out.shape[0]
    tm = _tile(seq, 512)
    tf = _tile(d_ff, 1408)
    nf = d_ff // tf
    final = final_g is not None
    xspec = pl.BlockSpec((None, tm, d), lambda b, i, j: (b, i, 0))
    in_specs = [
        xspec,
        _mod_spec(l, 3 * sub + 0, d), _mod_spec(l, 3 * sub + 1, d), _mod_spec(l, 3 * sub + 2, d),
        _const_spec((1, d)),
        pl.BlockSpec((d, tf), lambda b, i, j: (0, j)),
        pl.BlockSpec((d, tf), lambda b, i, j: (0, j + nf)),
        pl.BlockSpec((tf, d), lambda b, i, j: (j, 0)),
    ]
    args = [x, mod5, mod5, mod5, norm_g.reshape(1, d), w_in, w_in, w_out]
    if final:
        in_specs.append(_const_spec((1, d)))
        args.append(final_g.reshape(1, d))
    return pl.pallas_call(
        functools.partial(_ffn_kernel, final=final),
        out_shape=jax.ShapeDtypeStruct((bsz, seq, d), F32),
        grid=(bsz, seq // tm, nf),
        in_specs=in_specs,
        out_specs=xspec,
        scratch_shapes=[pltpu.VMEM((tm, d), BF16), pltpu.VMEM((tm, d), F32)],
        compiler_params=_params(("parallel", "parallel", "arbitrary")),
        name="ffn",
    )(*args)


def _prep_kernel(x_ref, sh_ref, sc_ref, g_ref, wa_ref, wb_ref, mu_ref, w0_ref, a0_ref,
                 w2a2_ref, g2_ref, qg_ref, kvg_ref, wq_ref, wk_ref, wv_ref, cos_ref, sin_ref,
                 r_ref, k_ref, v_ref, w_ref, a_ref, gr_ref,
                 sq_ref, sk_ref, sv_ref, mq_ref, mk_ref, mv_ref,
                 carry_ref):
    i = pl.program_id(1)
    bw = BRANCH_W

    @pl.when(i == 0)
    def _():
        carry_ref[...] = jnp.zeros_like(carry_ref)

    h = _modulated_norm(x_ref[...], g_ref[...], sh_ref[...], sc_ref[...]).astype(BF16)
    tm = h.shape[0]

    p = _dot(h, wa_ref[...])
    row = lax.broadcasted_iota(jnp.int32, p.shape, 0)
    prev = jnp.where(row == 0, carry_ref[...], pltpu.roll(p, 1, 0))
    carry_ref[...] = p[tm - 1:tm, :]
    ps = p + (prev - p) * mu_ref[...]
    r_ref[...] = ps[:, 0:bw]
    k_ref[...] = ps[:, bw:2 * bw]
    v_ref[...] = ps[:, 2 * bw:3 * bw]
    lora = ps[:, 3 * bw:3 * bw + LANES]
    lane = lax.broadcasted_iota(jnp.int32, lora.shape, 1)
    lora = jnp.where(lane < DECAY_LORA, jnp.tanh(lora), lora).astype(BF16)
    wa = _dot(lora, w2a2_ref[...])
    w_log = -jax.nn.softplus(-(w0_ref[...] + wa[:, :bw])) - 0.5
    w_ref[...] = jnp.exp(-jnp.exp(w_log))
    a_ref[...] = jax.nn.sigmoid(a0_ref[...] + wa[:, bw:])
    dg = jax.nn.sigmoid(ps[:, 3 * bw + LANES:]).astype(BF16)
    gr_ref[...] = _dot(dg, g2_ref[...])

    pb = _dot(h, wb_ref[...])
    sq_ref[...] = (pb[:, 0:bw] * (HEAD_DIM ** -0.5)).astype(BF16)
    sk_ref[...] = pb[:, bw:2 * bw].astype(BF16)
    sv_ref[...] = pb[:, 2 * bw:3 * bw].astype(BF16)
    off = 3 * bw
    cq = _rms(pb[:, off:off + MLA_Q_LORA], qg_ref[...]).astype(BF16)
    off += MLA_Q_LORA
    ckv = _rms(pb[:, off:off + MLA_KV_LORA], kvg_ref[...]).astype(BF16)
    off += MLA_KV_LORA
    cos = cos_ref[...]
    sin = sin_ref[...]
    kr = pb[:, off:off + LANES] * cos + pb[:, off + LANES:off + 2 * LANES] * sin
    qall = _dot(cq, wq_ref[...])
    kall = _dot(ckv, wk_ref[...])
    mv_ref[...] = _dot(ckv, wv_ref[...]).astype(BF16)
    scale = (MLA_NOPE_DIM + MLA_ROPE_DIM) ** -0.5
    for hd in range(N_HEADS):
        q = qall[:, 2 * hd * LANES:(2 * hd + 1) * LANES] * cos + qall[:, (2 * hd + 1) * LANES:(2 * hd + 2) * LANES] * sin
        mq_ref[hd] = (q * scale).astype(BF16)
        mk_ref[hd] = (kall[:, hd * LANES:(hd + 1) * LANES] + kr).astype(BF16)


def _rotate_half_cols(w):
    half = MLA_ROPE_DIM // 2
    return jnp.concatenate([-w[..., half:], w[..., :half]], axis=-1)


def _prep_weights(mix_w_in, rwkv_mu, rwkv_w2, rwkv_a2, rwkv_g2, mla_w_uq, mla_w_ukv):
    d = mix_w_in.shape[0]
    bw = BRANCH_W
    wa = jnp.pad(mix_w_in[:, :RWKV_COLS], ((0, 0), (0, RWKV_PAD - RWKV_COLS))).astype(BF16)
    mu = jnp.pad(rwkv_mu, (0, RWKV_PAD - RWKV_COLS)).reshape(1, RWKV_PAD)
    o = RWKV_COLS
    qkv_cq_ckv = mix_w_in[:, o:o + 3 * bw + MLA_Q_LORA + MLA_KV_LORA]
    o += 3 * bw + MLA_Q_LORA + MLA_KV_LORA
    w_kr = mix_w_in[:, o:o + MLA_ROPE_DIM]
    o += MLA_ROPE_DIM
    z_lo = jnp.zeros((d, MLA_NOPE_DIM), F32)
    z_hi = jnp.zeros((d, LANES - MLA_NOPE_DIM - MLA_ROPE_DIM), F32)
    wb = jnp.concatenate([qkv_cq_ckv, z_lo, w_kr, z_hi, z_lo, _rotate_half_cols(w_kr), z_hi], axis=1).astype(BF16)
    w_gate = mix_w_in[:, o:].astype(BF16)
    w2a2 = jnp.zeros((LANES, 2 * bw), F32)
    w2a2 = w2a2.at[:DECAY_LORA, :bw].set(rwkv_w2).at[DECAY_LORA:, bw:].set(rwkv_a2).astype(BF16)
    g2 = jnp.pad(rwkv_g2, ((0, GATE_LORA_PAD - GATE_LORA), (0, 0))).astype(BF16)
    qd = MLA_NOPE_DIM + MLA_ROPE_DIM
    uq = mla_w_uq.reshape(MLA_Q_LORA, N_HEADS, qd)
    zq = jnp.zeros((MLA_Q_LORA, N_HEADS, LANES - qd), F32)
    zn = jnp.zeros((MLA_Q_LORA, N_HEADS, MLA_NOPE_DIM), F32)
    wq = jnp.concatenate([uq, zq, zn, _rotate_half_cols(uq[..., MLA_NOPE_DIM:]), zq], axis=-1)
    wq = wq.reshape(MLA_Q_LORA, N_HEADS * 2 * LANES).astype(BF16)
    ukv = mla_w_ukv.reshape(MLA_KV_LORA, N_HEADS, MLA_NOPE_DIM + MLA_V_DIM)
    wk = jnp.concatenate([ukv[..., :MLA_NOPE_DIM], jnp.zeros((MLA_KV_LORA, N_HEADS, LANES - MLA_NOPE_DIM), F32)], axis=-1)
    wk = wk.reshape(MLA_KV_LORA, N_HEADS * LANES).astype(BF16)
    wv = ukv[..., MLA_NOPE_DIM:].reshape(MLA_KV_LORA, N_HEADS * MLA_V_DIM).astype(BF16)
    return wa, wb, mu, w2a2, g2, wq, wk, wv, w_gate


def _prep_call(x, mod5, l, norm_g, pw, w0, a0, q_norm_g, kv_norm_g, cos_t, sin_t):
    bsz, seq, d = x.shape
    wa, wb, mu, w2a2, g2, wq, wk, wv, _ = pw
    bw = BRANCH_W
    tm = _tile(seq, 256)
    row = lambda b, i: (b, i, 0)
    xspec = pl.BlockSpec((None, tm, d), row)
    bspec = pl.BlockSpec((None, tm, bw), row)
    lspec = pl.BlockSpec((None, tm, LANES), row)
    hspec = pl.BlockSpec((None, N_HEADS, tm, LANES), lambda b, i: (b, 0, i, 0))
    f32_out = jax.ShapeDtypeStruct((bsz, seq, bw), F32)
    bf_out = jax.ShapeDtypeStruct((bsz, seq, bw), BF16)
    hd_out = jax.ShapeDtypeStruct((bsz, N_HEADS, seq, LANES), BF16)
    return pl.pallas_call(
        _prep_kernel,
        out_shape=(f32_out,) * 6 + (bf_out,) * 3 + (hd_out, hd_out, bf_out),
        grid=(bsz, seq // tm),
        in_specs=[
            xspec, _mod_spec(l, 3, d), _mod_spec(l, 4, d), _const_spec((1, d)),
            _const_spec(wa.shape), _const_spec(wb.shape), _const_spec((1, RWKV_PAD)),
            _const_spec((1, bw)), _const_spec((1, bw)),
            _const_spec(w2a2.shape), _const_spec(g2.shape),
            _const_spec((1, MLA_Q_LORA)), _const_spec((1, MLA_KV_LORA)),
            _const_spec(wq.shape), _const_spec(wk.shape), _const_spec(wv.shape),
            lspec, lspec,
        ],
        out_specs=(bspec,) * 9 + (hspec, hspec, bspec),
        scratch_shapes=[pltpu.VMEM((1, RWKV_PAD), F32)],
        compiler_params=_params(("parallel", "arbitrary")),
        name="mixer_prep",
    )(x, mod5, mod5, norm_g.reshape(1, d), wa, wb, mu, w0.reshape(1, bw), a0.reshape(1, bw),
      w2a2, g2, q_norm_g.reshape(1, -1), kv_norm_g.reshape(1, -1), wq, wk, wv, cos_t, sin_t)


def _sb_kernel(q_ref, k_ref, v_ref, tri_ref, o_ref, acc_ref, c_ref):
    qi = pl.program_id(2)
    jj = pl.program_id(3)
    t = q_ref.shape[0]

    @pl.when(jj == 0)
    def _():
        acc_ref[...] = jnp.zeros_like(acc_ref)
        c_ref[...] = jnp.zeros_like(c_ref)

    def tile(diag):
        q = q_ref[...]
        k = k_ref[...]
        v = v_ref[...]
        tri = tri_ref[...]
        lane = lax.broadcasted_iota(jnp.int32, q.shape, 1)
        if diag:
            rows = lax.broadcasted_iota(jnp.int32, (t, t), 0)
            cols = lax.broadcasted_iota(jnp.int32, (t, t), 1)
            mask = cols < rows
        for hd in range(2):
            qh = jnp.where((lane < HEAD_DIM) == (hd == 0), q, jnp.zeros_like(q))
            z = _dot_nt(qh, k)
            sp = jnp.maximum(z, 0.0) + jnp.log(1.0 + jnp.exp(-jnp.abs(z)))
            if diag:
                sp = jnp.where(mask, sp, 0.0)
            hi = sp.astype(BF16)
            lo = (sp - hi.astype(F32)).astype(BF16)
            after = _dot(hi, tri) + _dot(lo, tri)
            c = c_ref[hd]
            w = jnp.exp(z - sp - after - c)
            if diag:
                w = jnp.where(mask, w, 0.0)
            acc_ref[hd] += _dot(w.astype(BF16), v)
            c_ref[hd] = c + jnp.sum(sp, axis=1, keepdims=True)

    @pl.when(jj == 0)
    def _():
        tile(True)

    @pl.when((jj > 0) & (jj <= qi))
    def _():
        tile(False)

    @pl.when(jj == pl.num_programs(3) - 1)
    def _():
        lane = lax.broadcasted_iota(jnp.int32, o_ref.shape, 1)
        o_ref[...] = jnp.where(lane < HEAD_DIM, acc_ref[0], acc_ref[1]).astype(o_ref.dtype)


def _sb_call(q, k, v):
    bsz, seq, _ = q.shape
    t = _tile(seq, 256)
    nq = seq // t
    tri = (lax.broadcasted_iota(jnp.int32, (t, t), 0) > lax.broadcasted_iota(jnp.int32, (t, t), 1)).astype(BF16)
    qspec = pl.BlockSpec((None, t, LANES), lambda b, hp, i, jj: (b, i, hp))
    kspec = pl.BlockSpec((None, t, LANES), lambda b, hp, i, jj: (b, jnp.maximum(i - jj, 0), hp))
    return pl.pallas_call(
        _sb_kernel,
        out_shape=jax.ShapeDtypeStruct(q.shape, BF16),
        grid=(bsz, N_HEADS // 2, nq, nq),
        in_specs=[qspec, kspec, kspec, _const_spec((t, t))],
        out_specs=qspec,
        scratch_shapes=[pltpu.VMEM((2, t, LANES), F32), pltpu.VMEM((2, t, 1), F32)],
        compiler_params=_params(("parallel", "parallel", "parallel", "arbitrary")),
        name="stick_breaking",
    )(q, k, v, tri)


def _mla_kernel(q_ref, k_ref, v_ref, o_ref, acc_ref, m_ref, l_ref):
    qi = pl.program_id(2)
    jj = pl.program_id(3)
    t = q_ref.shape[1]

    @pl.when(jj == 0)
    def _():
        acc_ref[...] = jnp.zeros_like(acc_ref)
        m_ref[...] = jnp.full_like(m_ref, NEG_BIG)
        l_ref[...] = jnp.zeros_like(l_ref)

    def tile(diag):
        v = v_ref[...]
        if diag:
            rows = lax.broadcasted_iota(jnp.int32, (t, t), 0)
            cols = lax.broadcasted_iota(jnp.int32, (t, t), 1)
            mask = cols <= rows
        for hd in range(2):
            s = _dot_nt(q_ref[hd], k_ref[hd])
            if diag:
                s = jnp.where(mask, s, NEG_BIG)
            m_old = m_ref[hd]
            m_new = jnp.maximum(m_old, jnp.max(s, axis=1, keepdims=True))
            p = jnp.exp(s - m_new)
            alpha = jnp.exp(m_old - m_new)
            l_ref[hd] = alpha * l_ref[hd] + jnp.sum(p, axis=1, keepdims=True)
            acc_ref[hd] = alpha * acc_ref[hd] + _dot(p.astype(BF16), v)
            m_ref[hd] = m_new

    @pl.when(jj < qi)
    def _():
        tile(False)

    @pl.when(jj == qi)
    def _():
        tile(True)
        lane = lax.broadcasted_iota(jnp.int32, o_ref.shape, 1)
        out = jnp.where(lane < HEAD_DIM, acc_ref[0] / l_ref[0], acc_ref[1] / l_ref[1])
        o_ref[...] = out.astype(o_ref.dtype)


def _mla_call(q, k, v):
    bsz, _, seq, _ = q.shape
    t = _tile(seq, 256)
    nq = seq // t
    qspec = pl.BlockSpec((None, 2, t, LANES), lambda b, hp, i, jj: (b, hp, i, 0))
    kspec = pl.BlockSpec((None, 2, t, LANES), lambda b, hp, i, jj: (b, hp, jnp.minimum(jj, i), 0))
    vspec = pl.BlockSpec((None, t, LANES), lambda b, hp, i, jj: (b, jnp.minimum(jj, i), hp))
    ospec = pl.BlockSpec((None, t, LANES), lambda b, hp, i, jj: (b, i, hp))
    return pl.pallas_call(
        _mla_kernel,
        out_shape=jax.ShapeDtypeStruct(v.shape, BF16),
        grid=(bsz, N_HEADS // 2, nq, nq),
        in_specs=[qspec, kspec, vspec],
        out_specs=ospec,
        scratch_shapes=[pltpu.VMEM((2, t, LANES), F32), pltpu.VMEM((2, t, 1), F32), pltpu.VMEM((2, t, 1), F32)],
        compiler_params=_params(("parallel", "parallel", "parallel", "arbitrary")),
        name="latent_attention",
    )(q, k, v)


def _scan_kernel(r_ref, w_ref, k_ref, v_ref, a_ref, kk_ref, ka_ref, rk_ref, lg_ref, lb_ref,
                 o_ref, h_ref):
    n = HEAD_DIM

    @pl.when(pl.program_id(0) == 0)
    def _():
        h_ref[...] = jnp.zeros_like(h_ref)

    k_k = kk_ref[...]
    k_a = ka_ref[...]
    r_k = rk_ref[...]
    ln_g = lg_ref[...]
    ln_b = lb_ref[...]

    def step(t, carry):
        rt = r_ref[t]
        wt = w_ref[t]
        kt = k_ref[t]
        vt = v_ref[t]
        at = a_ref[t]
        kk = kt * k_k
        kk = kk * lax.rsqrt(jnp.maximum(jnp.sum(kk * kk, axis=0, keepdims=True), 1e-24))
        km = kt * (1.0 + (at - 1.0) * k_a)
        bt = kk * at
        sa = jnp.zeros_like(vt)
        for c in range(n):
            sa = sa - h_ref[c] * kk[c:c + 1, :]
        y = jnp.zeros_like(vt)
        for c in range(n):
            hn = h_ref[c] * wt[c:c + 1, :] + sa * bt[c:c + 1, :] + vt * km[c:c + 1, :]
            h_ref[c] = hn
            y = y + hn * rt[c:c + 1, :]
        mean = jnp.mean(y, axis=0, keepdims=True)
        yc = y - mean
        var = jnp.mean(yc * yc, axis=0, keepdims=True)
        yn = yc * lax.rsqrt(var + RWKV_GN_EPS) * ln_g + ln_b
        bonus = jnp.sum(rt * km * r_k, axis=0, keepdims=True) * vt
        o_ref[t] = yn + bonus
        return carry

    lax.fori_loop(0, r_ref.shape[0], step, 0)


def _to_chains(t, lanes):
    bsz, seq, _ = t.shape
    t = t.reshape(bsz, seq, N_HEADS, HEAD_DIM).transpose(1, 3, 0, 2).reshape(seq, HEAD_DIM, bsz * N_HEADS)
    return jnp.pad(t, ((0, 0), (0, 0), (0, lanes - bsz * N_HEADS)))


def _param_chains(p, bsz, lanes):
    m = jnp.tile(p.reshape(N_HEADS, HEAD_DIM).T, (1, bsz))
    return jnp.pad(m, ((0, 0), (0, lanes - bsz * N_HEADS)))


def _scan_call(r, w, k, v, a, k_k, k_a, r_k, lnx_g, lnx_b):
    bsz, seq, _ = r.shape
    chains = bsz * N_HEADS
    assert chains <= LANES
    ins = [_to_chains(t, LANES) for t in (r, w, k, v, a)]
    prm = [_param_chains(p.reshape(-1), bsz, LANES) for p in (k_k, k_a, r_k, lnx_g, lnx_b)]
    ts = _tile(seq, 32)
    tspec = pl.BlockSpec((ts, HEAD_DIM, LANES), lambda i: (i, 0, 0))
    out = pl.pallas_call(
        _scan_kernel,
        out_shape=jax.ShapeDtypeStruct((seq, HEAD_DIM, LANES), F32),
        grid=(seq // ts,),
        in_specs=[tspec] * 5 + [_const_spec((HEAD_DIM, LANES))] * 5,
        out_specs=tspec,
        scratch_shapes=[pltpu.VMEM((HEAD_DIM, HEAD_DIM, LANES), F32)],
        compiler_params=_params(("arbitrary",)),
        name="rwkv7_scan",
    )(*ins, *prm)
    out = out[:, :, :chains].reshape(seq, HEAD_DIM, bsz, N_HEADS).transpose(2, 0, 3, 1)
    return out.reshape(bsz, seq, BRANCH_W)


def _merge_kernel(x_ref, sh_ref, sc_ref, gt_ref, g_ref, wg_ref, ya_ref, gr_ref, yb_ref, yc_ref,
                  bw_ref, wo_ref, o_ref):
    x = x_ref[...]
    d = x.shape[1]
    h = _modulated_norm(x, g_ref[...], sh_ref[...], sc_ref[...]).astype(BF16)
    ys = ((ya_ref[...] * gr_ref[...]).astype(BF16), yb_ref[...], yc_ref[...])
    merged = None
    for n in range(N_BRANCHES):
        gate = jax.nn.sigmoid(_dot(h, wg_ref[:, n * d:(n + 1) * d]))
        term = gate * _dot(ys[n], bw_ref[n])
        merged = term if merged is None else merged + term
    o_ref[...] = x + gt_ref[...] * _dot(merged.astype(BF16), wo_ref[...])


def _merge_call(x, mod5, l, norm_g, w_gate, ya, gr, yb, yc, branch_w, w_out):
    bsz, seq, d = x.shape
    bw = BRANCH_W
    tm = _tile(seq, 512)
    row = lambda b, i: (b, i, 0)
    xspec = pl.BlockSpec((None, tm, d), row)
    bspec = pl.BlockSpec((None, tm, bw), row)
    return pl.pallas_call(
        _merge_kernel,
        out_shape=jax.ShapeDtypeStruct(x.shape, F32),
        grid=(bsz, seq // tm),
        in_specs=[
            xspec, _mod_spec(l, 3, d), _mod_spec(l, 4, d), _mod_spec(l, 5, d), _const_spec((1, d)),
            _const_spec(w_gate.shape), bspec, bspec, bspec, bspec,
            _const_spec(branch_w.shape), _const_spec(w_out.shape),
        ],
        out_specs=xspec,
        compiler_params=_params(("parallel", "parallel")),
        name="mixer_merge",
    )(x, mod5, mod5, mod5, norm_g.reshape(1, d), w_gate, ya, gr, yb, yc, branch_w, w_out)


def kernel(x, c, positions, ada_w, ada_b, norm_g, ffn1_w_in, ffn1_w_out, mix_w_in, rwkv_mu, rwkv_w0, rwkv_w2, rwkv_a0, rwkv_a2, rwkv_g2, rwkv_k_k, rwkv_k_a, rwkv_r_k, rwkv_lnx_g, rwkv_lnx_b, mla_q_norm_g, mla_w_uq, mla_kv_norm_g, mla_w_ukv, branch_w, mix_w_out, ffn2_w_in, ffn2_w_out, final_norm_g):
    depth = ada_w.shape[0]
    mod5 = _ada_call(c, ada_w, ada_b)
    cos_t, sin_t = _rope_call(positions)
    for l in range(depth):
        x = _ffn_call(x, mod5, l, 0, norm_g[l, 0], ffn1_w_in[l].astype(BF16), ffn1_w_out[l].astype(BF16))
        pw = _prep_weights(mix_w_in[l], rwkv_mu[l], rwkv_w2[l], rwkv_a2[l], rwkv_g2[l], mla_w_uq[l], mla_w_ukv[l])
        (r, k, v, w, a, gr, sq, sk, sv, mq, mk, mv) = _prep_call(
            x, mod5, l, norm_g[l, 1], pw, rwkv_w0[l], rwkv_a0[l], mla_q_norm_g[l], mla_kv_norm_g[l], cos_t, sin_t)
        ya = _scan_call(r, w, k, v, a, rwkv_k_k[l], rwkv_k_a[l], rwkv_r_k[l], rwkv_lnx_g[l], rwkv_lnx_b[l])
        yb = _sb_call(sq, sk, sv)
        yc = _mla_call(mq, mk, mv)
        x = _merge_call(x, mod5, l, norm_g[l, 1], pw[-1], ya, gr, yb, yc,
                        branch_w[l].astype(BF16), mix_w_out[l].astype(BF16))
        x = _ffn_call(x, mod5, l, 2, norm_g[l, 2], ffn2_w_in[l].astype(BF16), ffn2_w_out[l].astype(BF16),
                      final_g=final_norm_g if l == depth - 1 else None)
    return x
```

```python
import functools
import math

import jax
import jax.numpy as jnp
from jax import lax
from jax.experimental import pallas as pl
from jax.experimental.pallas import tpu as pltpu

F32 = jnp.float32
BF16 = jnp.bfloat16

HEAD_DIM = 64
N_HEADS = 6
BRANCH_W = N_HEADS * HEAD_DIM
N_BRANCHES = 3
DECAY_LORA = 64
AAA_LORA = 64
GATE_LORA = 160
RWKV_GN_EPS = 64e-5
MLA_Q_LORA = 256
MLA_KV_LORA = 128
MLA_NOPE_DIM = 64
MLA_ROPE_DIM = 32
MLA_V_DIM = 64
ROPE_THETA = 10000.0
MACARON_WEIGHT = 0.5
NORM_EPS = 1e-6
N_ADA = 9
RWKV_COLS = 3 * BRANCH_W + DECAY_LORA + AAA_LORA + GATE_LORA

LANES = 128
SUBLANES = 8
MXU_WIDTH = 256
VMEM_LIMIT_BYTES = 56 * 1024 * 1024

FFN_ROWS = 512
MERGE_ROWS = 512
ATT_TILE = 256
SCAN_STEPS = 32

RWKV_PAD = 1536
GATE_LORA_PAD = RWKV_PAD - (3 * BRANCH_W + DECAY_LORA + AAA_LORA)
NEG_BIG = -1e30
LOG2E = math.log2(math.e)


def _tile(n, pref):
    t = min(n, pref)
    assert n % t == 0, (n, t)
    return t


def _params(sem):
    return pltpu.CompilerParams(dimension_semantics=sem, vmem_limit_bytes=VMEM_LIMIT_BYTES)


def _const_spec(shape):
    nd = len(shape)
    return pl.BlockSpec(shape, lambda *_: (0,) * nd)


def _resident_spec(shape):
    nd = len(shape)
    return pl.BlockSpec(shape, lambda *_: (0,) * nd, pipeline_mode=pl.Buffered(1))


def _mod_spec(l, j, d):
    return pl.BlockSpec((None, None, None, 1, d), lambda b, *_: (l, j, b, 0, 0))


def _modulated_norm(x, g, shift, scale):
    ms = jnp.mean(x * x, axis=-1, keepdims=True)
    y = x * lax.rsqrt(ms + NORM_EPS) * g
    return y * (1.0 + scale) + shift


def _rms(x, g):
    ms = jnp.mean(x * x, axis=-1, keepdims=True)
    return x * lax.rsqrt(ms + NORM_EPS) * g


def _dot(a, b):
    return jnp.dot(a, b, preferred_element_type=F32)


def _ada_kernel(c_ref, w_ref, b_ref, o_ref):
    c = c_ref[...]
    ca = (c * jax.nn.sigmoid(c)).astype(BF16)
    o_ref[...] = _dot(ca, w_ref[...].astype(BF16)) + b_ref[...]


def _ada_call(c, ada_w, ada_b):
    depth, d, _ = ada_w.shape
    bsz = c.shape[0]
    ada_b4 = ada_b.reshape(depth, N_ADA, 1, d)
    out = pl.pallas_call(
        _ada_kernel,
        out_shape=jax.ShapeDtypeStruct((depth, N_ADA, bsz, d), F32),
        grid=(depth, N_ADA),
        in_specs=[
            pl.BlockSpec((bsz, d), lambda l, j: (0, 0)),
            pl.BlockSpec((None, d, d), lambda l, j: (l, 0, j)),
            pl.BlockSpec((None, None, 1, d), lambda l, j: (l, j, 0, 0)),
        ],
        out_specs=pl.BlockSpec((None, None, bsz, d), lambda l, j: (l, j, 0, 0)),
        compiler_params=_params(("parallel", "parallel")),
        name="ada_mod",
    )(c, ada_w, ada_b4)
    return out.reshape(depth, N_ADA, bsz, 1, d)


def _rope_kernel(pos_ref, freq_ref, cos_ref, sin_ref):
    ang = pos_ref[...].astype(F32) * freq_ref[...]
    lane = lax.broadcasted_iota(jnp.int32, ang.shape, 1)
    in_rope = (lane >= MLA_NOPE_DIM) & (lane < MLA_NOPE_DIM + MLA_ROPE_DIM)
    cos_ref[...] = jnp.where(lane < MLA_NOPE_DIM, 1.0, jnp.where(in_rope, jnp.cos(ang), 0.0))
    sin_ref[...] = jnp.where(in_rope, jnp.sin(ang), 0.0)


def _rope_call(positions):
    bsz, seq = positions.shape
    half = MLA_ROPE_DIM // 2
    inv_freq = 1.0 / (ROPE_THETA ** (jnp.arange(0, MLA_ROPE_DIM, 2, dtype=F32) / MLA_ROPE_DIM))
    freq = jnp.zeros((1, LANES), F32)
    freq = freq.at[0, MLA_NOPE_DIM:MLA_NOPE_DIM + half].set(inv_freq)
    freq = freq.at[0, MLA_NOPE_DIM + half:MLA_NOPE_DIM + 2 * half].set(inv_freq)
    ts = _tile(seq, 512)
    spec = pl.BlockSpec((None, ts, LANES), lambda b, i: (b, i, 0))
    return pl.pallas_call(
        _rope_kernel,
        out_shape=(jax.ShapeDtypeStruct((bsz, seq, LANES), F32),) * 2,
        grid=(bsz, seq // ts),
        in_specs=[pl.BlockSpec((None, ts, 1), lambda b, i: (b, i, 0)), _const_spec((1, LANES))],
        out_specs=(spec, spec),
        compiler_params=_params(("parallel", "parallel")),
        name="rope_tables",
    )(positions.reshape(bsz, seq, 1), freq)


def _ffn_kernel(x_ref, sh_ref, sc_ref, gt_ref, g_ref, wi_ref, wo_ref, *rest, final):
    if final:
        fg_ref, o_ref = rest
    else:
        (o_ref,) = rest
    d_ff = wo_ref.shape[0]
    x = x_ref[...]
    h = _modulated_norm(x, g_ref[...], sh_ref[...], sc_ref[...]).astype(BF16)
    acc = None
    for c0 in range(0, d_ff, MXU_WIDTH):
        gate = _dot(h, wi_ref[:, c0:c0 + MXU_WIDTH])
        up = _dot(h, wi_ref[:, d_ff + c0:d_ff + c0 + MXU_WIDTH])
        act = (gate * jax.nn.sigmoid(gate) * up).astype(BF16)
        part = _dot(act, wo_ref[c0:c0 + MXU_WIDTH, :])
        acc = part if acc is None else acc + part
    out = x + (MACARON_WEIGHT * gt_ref[...]) * acc
    if final:
        out = _rms(out, fg_ref[...])
    o_ref[...] = out


def _ffn_call(x, mod5, l, sub, norm_g, w_in, w_out, final_g=None):
    bsz, seq, d = x.shape
    d_ff = w_out.shape[0]
    assert d_ff % MXU_WIDTH == 0
    tm = _tile(seq, FFN_ROWS)
    final = final_g is not None
    xspec = pl.BlockSpec((None, tm, d), lambda b, i: (b, i, 0))
    in_specs = [
        xspec,
        _mod_spec(l, 3 * sub + 0, d), _mod_spec(l, 3 * sub + 1, d), _mod_spec(l, 3 * sub + 2, d),
        _const_spec((1, d)),
        _resident_spec(w_in.shape), _resident_spec(w_out.shape),
    ]
    args = [x, mod5, mod5, mod5, norm_g.reshape(1, d), w_in, w_out]
    if final:
        in_specs.append(_const_spec((1, d)))
        args.append(final_g.reshape(1, d))
    return pl.pallas_call(
        functools.partial(_ffn_kernel, final=final),
        out_shape=jax.ShapeDtypeStruct((bsz, seq, d), F32),
        grid=(bsz, seq // tm),
        in_specs=in_specs,
        out_specs=xspec,
        compiler_params=_params(("parallel", "parallel")),
        name="ffn",
    )(*args)


def _prep_kernel(x_ref, sh_ref, sc_ref, g_ref, wa_ref, wb_ref, mu_ref, w0_ref, a0_ref,
                 w2a2_ref, g2_ref, qg_ref, kvg_ref, wq_ref, wk_ref, wv_ref, cos_ref, sin_ref,
                 r_ref, k_ref, v_ref, w_ref, a_ref, gr_ref,
                 sqt_ref, sk_ref, svt_ref, mqt_ref, mk_ref, mvt_ref,
                 carry_ref):
    i = pl.program_id(1)
    bw = BRANCH_W

    @pl.when(i == 0)
    def _():
        carry_ref[...] = jnp.zeros_like(carry_ref)

    h = _modulated_norm(x_ref[...], g_ref[...], sh_ref[...], sc_ref[...]).astype(BF16)
    tm = h.shape[0]

    p = _dot(h, wa_ref[...])
    row = lax.broadcasted_iota(jnp.int32, p.shape, 0)
    prev = jnp.where(row == 0, carry_ref[...], pltpu.roll(p, 1, 0))
    carry_ref[...] = p[tm - 1:tm, :]
    ps = p + (prev - p) * mu_ref[...]
    r_ref[...] = ps[:, 0:bw]
    k_ref[...] = ps[:, bw:2 * bw]
    v_ref[...] = ps[:, 2 * bw:3 * bw]
    lora = ps[:, 3 * bw:3 * bw + LANES]
    lane = lax.broadcasted_iota(jnp.int32, lora.shape, 1)
    lora = jnp.where(lane < DECAY_LORA, jnp.tanh(lora), lora).astype(BF16)
    wa = _dot(lora, w2a2_ref[...])
    w_log = -jax.nn.softplus(-(w0_ref[...] + wa[:, :bw])) - 0.5
    w_ref[...] = jnp.exp(-jnp.exp(w_log))
    a_ref[...] = jax.nn.sigmoid(a0_ref[...] + wa[:, bw:])
    dg = jax.nn.sigmoid(ps[:, 3 * bw + LANES:]).astype(BF16)
    gr_ref[...] = _dot(dg, g2_ref[...])

    pb = _dot(h, wb_ref[...])
    sqt_ref[...] = (pb[:, 0:bw] * (HEAD_DIM ** -0.5 * LOG2E)).T.astype(BF16)
    sk_ref[...] = pb[:, bw:2 * bw].astype(BF16)
    svt_ref[...] = pb[:, 2 * bw:3 * bw].T.astype(BF16)
    off = 3 * bw
    cq = _rms(pb[:, off:off + MLA_Q_LORA], qg_ref[...]).astype(BF16)
    off += MLA_Q_LORA
    ckv = _rms(pb[:, off:off + MLA_KV_LORA], kvg_ref[...]).astype(BF16)
    off += MLA_KV_LORA
    cos = cos_ref[...]
    sin = sin_ref[...]
    kr = pb[:, off:off + LANES] * cos + pb[:, off + LANES:off + 2 * LANES] * sin
    qall = _dot(cq, wq_ref[...])
    kall = _dot(ckv, wk_ref[...])
    mvt_ref[...] = _dot(ckv, wv_ref[...]).T.astype(BF16)
    scale = (MLA_NOPE_DIM + MLA_ROPE_DIM) ** -0.5 * LOG2E
    for hd in range(N_HEADS):
        q = qall[:, 2 * hd * LANES:(2 * hd + 1) * LANES] * cos + qall[:, (2 * hd + 1) * LANES:(2 * hd + 2) * LANES] * sin
        mqt_ref[hd * LANES:(hd + 1) * LANES, :] = (q * scale).T.astype(BF16)
        mk_ref[hd] = (kall[:, hd * LANES:(hd + 1) * LANES] + kr).astype(BF16)


def _rotate_half_cols(w):
    half = MLA_ROPE_DIM // 2
    return jnp.concatenate([-w[..., half:], w[..., :half]], axis=-1)


def _prep_weights(mix_w_in, rwkv_mu, rwkv_w2, rwkv_a2, rwkv_g2, mla_w_uq, mla_w_ukv):
    d = mix_w_in.shape[0]
    bw = BRANCH_W
    wa = jnp.pad(mix_w_in[:, :RWKV_COLS], ((0, 0), (0, RWKV_PAD - RWKV_COLS))).astype(BF16)
    mu = jnp.pad(rwkv_mu, (0, RWKV_PAD - RWKV_COLS)).reshape(1, RWKV_PAD)
    o = RWKV_COLS
    qkv_cq_ckv = mix_w_in[:, o:o + 3 * bw + MLA_Q_LORA + MLA_KV_LORA]
    o += 3 * bw + MLA_Q_LORA + MLA_KV_LORA
    w_kr = mix_w_in[:, o:o + MLA_ROPE_DIM]
    o += MLA_ROPE_DIM
    z_lo = jnp.zeros((d, MLA_NOPE_DIM), F32)
    z_hi = jnp.zeros((d, LANES - MLA_NOPE_DIM - MLA_ROPE_DIM), F32)
    wb = jnp.concatenate([qkv_cq_ckv, z_lo, w_kr, z_hi, z_lo, _rotate_half_cols(w_kr), z_hi], axis=1).astype(BF16)
    w_gate = mix_w_in[:, o:].astype(BF16)
    w2a2 = jnp.zeros((LANES, 2 * bw), F32)
    w2a2 = w2a2.at[:DECAY_LORA, :bw].set(rwkv_w2).at[DECAY_LORA:, bw:].set(rwkv_a2).astype(BF16)
    g2 = jnp.pad(rwkv_g2, ((0, GATE_LORA_PAD - GATE_LORA), (0, 0))).astype(BF16)
    qd = MLA_NOPE_DIM + MLA_ROPE_DIM
    uq = mla_w_uq.reshape(MLA_Q_LORA, N_HEADS, qd)
    zq = jnp.zeros((MLA_Q_LORA, N_HEADS, LANES - qd), F32)
    zn = jnp.zeros((MLA_Q_LORA, N_HEADS, MLA_NOPE_DIM), F32)
    wq = jnp.concatenate([uq, zq, zn, _rotate_half_cols(uq[..., MLA_NOPE_DIM:]), zq], axis=-1)
    wq = wq.reshape(MLA_Q_LORA, N_HEADS * 2 * LANES).astype(BF16)
    ukv = mla_w_ukv.reshape(MLA_KV_LORA, N_HEADS, MLA_NOPE_DIM + MLA_V_DIM)
    wk = jnp.concatenate([ukv[..., :MLA_NOPE_DIM], jnp.zeros((MLA_KV_LORA, N_HEADS, LANES - MLA_NOPE_DIM), F32)], axis=-1)
    wk = wk.reshape(MLA_KV_LORA, N_HEADS * LANES).astype(BF16)
    wv = ukv[..., MLA_NOPE_DIM:].reshape(MLA_KV_LORA, N_HEADS * MLA_V_DIM).astype(BF16)
    return wa, wb, mu, w2a2, g2, wq, wk, wv, w_gate


def _prep_call(x, mod5, l, norm_g, pw, w0, a0, q_norm_g, kv_norm_g, cos_t, sin_t):
    bsz, seq, d = x.shape
    wa, wb, mu, w2a2, g2, wq, wk, wv, _ = pw
    bw = BRANCH_W
    tm = _tile(seq, ATT_TILE)
    nt = seq // tm
    row = lambda b, i: (b, i, 0)
    col = lambda b, i: (b, 0, i)
    xspec = pl.BlockSpec((None, tm, d), row)
    bspec = pl.BlockSpec((None, tm, bw), row)
    lspec = pl.BlockSpec((None, tm, LANES), row)
    f32_out = jax.ShapeDtypeStruct((bsz, seq, bw), F32)
    out_shape = (f32_out,) * 6 + (
        jax.ShapeDtypeStruct((bsz, bw, seq), BF16),
        jax.ShapeDtypeStruct((bsz, seq, bw), BF16),
        jax.ShapeDtypeStruct((bsz, nt, bw, tm), BF16),
        jax.ShapeDtypeStruct((bsz, N_HEADS * LANES, seq), BF16),
        jax.ShapeDtypeStruct((bsz, N_HEADS, seq, LANES), BF16),
        jax.ShapeDtypeStruct((bsz, nt, bw, tm), BF16),
    )
    vt_spec = pl.BlockSpec((None, None, bw, tm), lambda b, i: (b, i, 0, 0))
    out_specs = (bspec,) * 6 + (
        pl.BlockSpec((None, bw, tm), col),
        bspec,
        vt_spec,
        pl.BlockSpec((None, N_HEADS * LANES, tm), col),
        pl.BlockSpec((None, N_HEADS, tm, LANES), lambda b, i: (b, 0, i, 0)),
        vt_spec,
    )
    return pl.pallas_call(
        _prep_kernel,
        out_shape=out_shape,
        grid=(bsz, nt),
        in_specs=[
            xspec, _mod_spec(l, 3, d), _mod_spec(l, 4, d), _const_spec((1, d)),
            _resident_spec(wa.shape), _resident_spec(wb.shape), _const_spec((1, RWKV_PAD)),
            _const_spec((1, bw)), _const_spec((1, bw)),
            _resident_spec(w2a2.shape), _resident_spec(g2.shape),
            _const_spec((1, MLA_Q_LORA)), _const_spec((1, MLA_KV_LORA)),
            _resident_spec(wq.shape), _resident_spec(wk.shape), _resident_spec(wv.shape),
            lspec, lspec,
        ],
        out_specs=out_specs,
        scratch_shapes=[pltpu.VMEM((1, RWKV_PAD), F32)],
        compiler_params=_params(("parallel", "arbitrary")),
        name="mixer_prep",
    )(x, mod5, mod5, norm_g.reshape(1, d), wa, wb, mu, w0.reshape(1, bw), a0.reshape(1, bw),
      w2a2, g2, q_norm_g.reshape(1, -1), kv_norm_g.reshape(1, -1), wq, wk, wv, cos_t, sin_t)


def _sb_kernel(qt_ref, k_ref, vt_ref, tri_ref, o_ref, acc_ref, c_ref):
    qi = pl.program_id(1)
    tk = vt_ref.shape[2]
    tq = qt_ref.shape[1]
    heads = range(N_HEADS)
    frow = lax.broadcasted_iota(jnp.int32, (LANES, tq), 0)
    q_heads = []
    for hd in heads:
        qt = qt_ref[(hd // 2) * LANES:(hd // 2 + 1) * LANES, :]
        keep = frow < HEAD_DIM if hd % 2 == 0 else frow >= HEAD_DIM
        q_heads.append(jnp.where(keep, qt, jnp.zeros_like(qt)))
    acc_ref[...] = jnp.zeros_like(acc_ref)
    c_ref[...] = jnp.zeros_like(c_ref)

    def tile(j, diag):
        k = k_ref[pl.ds(pl.multiple_of(j * tk, tk), tk), :]
        vt = vt_ref[j]
        tri = tri_ref[...]
        if diag:
            key = lax.broadcasted_iota(jnp.int32, (tk, tq), 0)
            qry = lax.broadcasted_iota(jnp.int32, (tk, tq), 1)
            mask = key < qry
        zs = [_dot(k[:, (hd // 2) * LANES:(hd // 2 + 1) * LANES], q_heads[hd]) for hd in heads]
        sps, lss = [], []
        for z in zs:
            t = jnp.log2(1.0 + jnp.exp2(-jnp.abs(z)))
            sp = jnp.maximum(z, 0.0) + t
            if diag:
                sp = jnp.where(mask, sp, 0.0)
            sps.append(sp)
            lss.append(jnp.minimum(z, 0.0) - t)
        afters = []
        for sp in sps:
            hi = sp.astype(BF16)
            lo = (sp - hi.astype(F32)).astype(BF16)
            afters.append(_dot(tri, jnp.concatenate([hi, lo], axis=0)))
        ws = []
        for hd in heads:
            c = c_ref[hd]
            w = jnp.exp2(lss[hd] - afters[hd][:tk] - c)
            if diag:
                w = jnp.where(mask, w, 0.0)
            ws.append(w.astype(BF16))
            c_ref[hd] = c + afters[hd][tk:tk + 1]
        for hd in heads:
            rows = slice(hd * HEAD_DIM, (hd + 1) * HEAD_DIM)
            acc_ref[rows, :] += _dot(vt[rows, :], ws[hd])

    tile(qi, True)

    def body(jj, carry):
        tile(qi - 1 - jj, False)
        return carry

    lax.fori_loop(0, qi, body, 0)
    o_ref[...] = acc_ref[...].T.astype(o_ref.dtype)


def _sb_call(qt, k, vt):
    bsz, seq, bw = k.shape
    nk, tk = vt.shape[1], vt.shape[3]
    tq = tk
    tri = lax.broadcasted_iota(jnp.int32, (tk, tk), 1) > lax.broadcasted_iota(jnp.int32, (tk, tk), 0)
    tri = jnp.concatenate([tri, jnp.ones((SUBLANES, tk), jnp.bool_)], axis=0).astype(BF16)
    tri = jnp.concatenate([tri, tri], axis=1)
    return pl.pallas_call(
        _sb_kernel,
        out_shape=jax.ShapeDtypeStruct(k.shape, BF16),
        grid=(bsz, seq // tq),
        in_specs=[
            pl.BlockSpec((None, bw, tq), lambda b, i: (b, 0, i)),
            pl.BlockSpec((None, seq, bw), lambda b, i: (b, 0, 0)),
            pl.BlockSpec((None, nk, bw, tk), lambda b, i: (b, 0, 0, 0)),
            _const_spec(tri.shape),
        ],
        out_specs=pl.BlockSpec((None, tq, bw), lambda b, i: (b, i, 0)),
        scratch_shapes=[pltpu.VMEM((bw, tq), F32), pltpu.VMEM((N_HEADS, 1, tq), F32)],
        compiler_params=_params(("parallel", "parallel")),
        name="stick_breaking",
    )(qt, k, vt, tri)


def _mla_kernel(qt_ref, k_ref, vt_ref, o_ref, acc_ref, m_ref, l_ref):
    qi = pl.program_id(1)
    tk = vt_ref.shape[2]
    tq = qt_ref.shape[1]
    heads = range(N_HEADS)
    acc_ref[...] = jnp.zeros_like(acc_ref)
    m_ref[...] = jnp.full_like(m_ref, NEG_BIG)
    l_ref[...] = jnp.zeros_like(l_ref)

    def tile(j, diag):
        start = pl.multiple_of(j * tk, tk)
        vt = vt_ref[j]
        if diag:
            key = lax.broadcasted_iota(jnp.int32, (tk, tq), 0)
            qry = lax.broadcasted_iota(jnp.int32, (tk, tq), 1)
            mask = key <= qry
        ss = [_dot(k_ref[hd, pl.ds(start, tk), :], qt_ref[hd * LANES:(hd + 1) * LANES, :]) for hd in heads]
        if diag:
            ss = [jnp.where(mask, s, NEG_BIG) for s in ss]
        ps, alphas = [], []
        for hd in heads:
            m_old = m_ref[hd]
            m_new = jnp.maximum(m_old, jnp.max(ss[hd], axis=0, keepdims=True))
            p = jnp.exp2(ss[hd] - m_new)
            alpha = jnp.exp2(m_old - m_new)
            l_ref[hd] = alpha * l_ref[hd] + jnp.sum(p, axis=0, keepdims=True)
            m_ref[hd] = m_new
            ps.append(p.astype(BF16))
            alphas.append(alpha)
        for hd in heads:
            rows = slice(hd * HEAD_DIM, (hd + 1) * HEAD_DIM)
            acc_ref[rows, :] = alphas[hd] * acc_ref[rows, :] + _dot(vt[rows, :], ps[hd])

    tile(qi, True)

    def body(j, carry):
        tile(j, False)
        return carry

    lax.fori_loop(0, qi, body, 0)
    out = jnp.concatenate(
        [acc_ref[hd * HEAD_DIM:(hd + 1) * HEAD_DIM, :] / l_ref[hd] for hd in heads], axis=0)
    o_ref[...] = out.T.astype(o_ref.dtype)


def _mla_call(qt, k, vt):
    bsz, _, seq, _ = k.shape
    nk, bw, tk = vt.shape[1:]
    tq = tk
    return pl.pallas_call(
        _mla_kernel,
        out_shape=jax.ShapeDtypeStruct((bsz, seq, bw), BF16),
        grid=(bsz, seq // tq),
        in_specs=[
            pl.BlockSpec((None, N_HEADS * LANES, tq), lambda b, i: (b, 0, i)),
            pl.BlockSpec((None, N_HEADS, seq, LANES), lambda b, i: (b, 0, 0, 0)),
            pl.BlockSpec((None, nk, bw, tk), lambda b, i: (b, 0, 0, 0)),
        ],
        out_specs=pl.BlockSpec((None, tq, bw), lambda b, i: (b, i, 0)),
        scratch_shapes=[pltpu.VMEM((bw, tq), F32), pltpu.VMEM((N_HEADS, 1, tq), F32),
                        pltpu.VMEM((N_HEADS, 1, tq), F32)],
        compiler_params=_params(("parallel", "parallel")),
        name="latent_attention",
    )(qt, k, vt)


def _scan_kernel(r_ref, w_ref, k_ref, v_ref, a_ref, kk_ref, ka_ref, rk_ref, lg_ref, lb_ref,
                 o_ref, h_ref, vec_ref):
    n = HEAD_DIM

    @pl.when(pl.program_id(0) == 0)
    def _():
        h_ref[...] = jnp.zeros_like(h_ref)

    def bcast_row(ref, c):
        return ref[pl.ds(c, 1), :]

    def step(t, carry):
        kt = k_ref[t]
        at = a_ref[t]
        vt = v_ref[t]
        kk = kt * kk_ref[...]
        kk = kk * lax.rsqrt(jnp.maximum(jnp.sum(kk * kk, axis=0, keepdims=True), 1e-24))
        km = kt * (1.0 + (at - 1.0) * ka_ref[...])
        vec_ref[0] = kk
        vec_ref[1] = kk * at
        vec_ref[2] = km
        bonus = jnp.sum(r_ref[t] * km * rk_ref[...], axis=0, keepdims=True)
        sa = jnp.zeros_like(vt)
        for c in range(n):
            sa = sa - h_ref[c] * bcast_row(vec_ref.at[0], c)
        y = jnp.zeros_like(vt)
        for c in range(n):
            hn = (h_ref[c] * bcast_row(w_ref.at[t], c) + sa * bcast_row(vec_ref.at[1], c)
                  + vt * bcast_row(vec_ref.at[2], c))
            h_ref[c] = hn
            y = y + hn * bcast_row(r_ref.at[t], c)
        mean = jnp.mean(y, axis=(0, 1), keepdims=True)
        yc = y - mean
        var = jnp.mean(yc * yc, axis=(0, 1), keepdims=True)
        yn = yc * lax.rsqrt(var + RWKV_GN_EPS) * lg_ref[...] + lb_ref[...]
        o_ref[t] = yn + bonus * vt
        return carry

    lax.fori_loop(0, r_ref.shape[0], step, 0)


def _to_chains(t):
    bsz, seq, _ = t.shape
    return t.reshape(bsz, seq, N_HEADS, HEAD_DIM).transpose(1, 3, 0, 2).reshape(seq, HEAD_DIM, bsz * N_HEADS)


def _param_chains(p, bsz):
    return jnp.tile(p.reshape(N_HEADS, HEAD_DIM).T, (1, bsz))


def _scan_call(r, w, k, v, a, k_k, k_a, r_k, lnx_g, lnx_b):
    bsz, seq, _ = r.shape
    chains = bsz * N_HEADS
    n = HEAD_DIM
    sub = n // SUBLANES
    r, w, k, v, a = (_to_chains(t) for t in (r, w, k, v, a))
    v = v.reshape(seq, sub, SUBLANES, chains)
    k_k, k_a, r_k, lnx_g, lnx_b = (_param_chains(p.reshape(-1), bsz) for p in (k_k, k_a, r_k, lnx_g, lnx_b))
    lnx_g = lnx_g.reshape(sub, SUBLANES, chains)
    lnx_b = lnx_b.reshape(sub, SUBLANES, chains)
    ts = _tile(seq, SCAN_STEPS)
    tspec = pl.BlockSpec((ts, n, chains), lambda i: (i, 0, 0))
    vspec = pl.BlockSpec((ts, sub, SUBLANES, chains), lambda i: (i, 0, 0, 0))
    pspec = _const_spec((n, chains))
    gspec = _const_spec((sub, SUBLANES, chains))
    out = pl.pallas_call(
        _scan_kernel,
        out_shape=jax.ShapeDtypeStruct((seq, sub, SUBLANES, chains), F32),
        grid=(seq // ts,),
        in_specs=[tspec, tspec, tspec, vspec, tspec, pspec, pspec, pspec, gspec, gspec],
        out_specs=vspec,
        scratch_shapes=[pltpu.VMEM((n, sub, SUBLANES, chains), F32), pltpu.VMEM((3, n, chains), F32)],
        compiler_params=_params(("arbitrary",)),
        name="rwkv7_scan",
    )(r, w, k, v, a, k_k, k_a, r_k, lnx_g, lnx_b)
    out = out.reshape(seq, n, bsz, N_HEADS).transpose(2, 0, 3, 1)
    return out.reshape(bsz, seq, BRANCH_W)


def _merge_kernel(x_ref, sh_ref, sc_ref, gt_ref, g_ref, wg_ref, ya_ref, gr_ref, yb_ref, yc_ref,
                  bw_ref, wo_ref, o_ref):
    x = x_ref[...]
    d = x.shape[1]
    h = _modulated_norm(x, g_ref[...], sh_ref[...], sc_ref[...]).astype(BF16)
    ys = ((ya_ref[...] * gr_ref[...]).astype(BF16), yb_ref[...], yc_ref[...])
    merged = None
    for n in range(N_BRANCHES):
        gate = jax.nn.sigmoid(_dot(h, wg_ref[:, n * d:(n + 1) * d]))
        term = gate * _dot(ys[n], bw_ref[n])
        merged = term if merged is None else merged + term
    o_ref[...] = x + gt_ref[...] * _dot(merged.astype(BF16), wo_ref[...])


def _merge_call(x, mod5, l, norm_g, w_gate, ya, gr, yb, yc, branch_w, w_out):
    bsz, seq, d = x.shape
    bw = BRANCH_W
    tm = _tile(seq, MERGE_ROWS)
    row = lambda b, i: (b, i, 0)
    xspec = pl.BlockSpec((None, tm, d), row)
    bspec = pl.BlockSpec((None, tm, bw), row)
    return pl.pallas_call(
        _merge_kernel,
        out_shape=jax.ShapeDtypeStruct(x.shape, F32),
        grid=(bsz, seq // tm),
        in_specs=[
            xspec, _mod_spec(l, 3, d), _mod_spec(l, 4, d), _mod_spec(l, 5, d), _const_spec((1, d)),
            _resident_spec(w_gate.shape), bspec, bspec, bspec, bspec,
            _resident_spec(branch_w.shape), _resident_spec(w_out.shape),
        ],
        out_specs=xspec,
        compiler_params=_params(("parallel", "parallel")),
        name="mixer_merge",
    )(x, mod5, mod5, mod5, norm_g.reshape(1, d), w_gate, ya, gr, yb, yc, branch_w, w_out)


def kernel(x, c, positions, ada_w, ada_b, norm_g, ffn1_w_in, ffn1_w_out, mix_w_in, rwkv_mu, rwkv_w0, rwkv_w2, rwkv_a0, rwkv_a2, rwkv_g2, rwkv_k_k, rwkv_k_a, rwkv_r_k, rwkv_lnx_g, rwkv_lnx_b, mla_q_norm_g, mla_w_uq, mla_kv_norm_g, mla_w_ukv, branch_w, mix_w_out, ffn2_w_in, ffn2_w_out, final_norm_g):
    depth = ada_w.shape[0]
    mod5 = _ada_call(c, ada_w, ada_b)
    cos_t, sin_t = _rope_call(positions)
    for l in range(depth):
        x = _ffn_call(x, mod5, l, 0, norm_g[l, 0], ffn1_w_in[l].astype(BF16), ffn1_w_out[l].astype(BF16))
        pw = _prep_weights(mix_w_in[l], rwkv_mu[l], rwkv_w2[l], rwkv_a2[l], rwkv_g2[l], mla_w_uq[l], mla_w_ukv[l])
        (r, k, v, w, a, gr, sqt, sk, svt, mqt, mk, mvt) = _prep_call(
            x, mod5, l, norm_g[l, 1], pw, rwkv_w0[l], rwkv_a0[l], mla_q_norm_g[l], mla_kv_norm_g[l], cos_t, sin_t)
        ya = _scan_call(r, w, k, v, a, rwkv_k_k[l], rwkv_k_a[l], rwkv_r_k[l], rwkv_lnx_g[l], rwkv_lnx_b[l])
        yb = _sb_call(sqt, sk, svt)
        yc = _mla_call(mqt, mk, mvt)
        x = _merge_call(x, mod5, l, norm_g[l, 1], pw[-1], ya, gr, yb, yc,
                        branch_w[l].astype(BF16), mix_w_out[l].astype(BF16))
        x = _ffn_call(x, mod5, l, 2, norm_g[l, 2], ffn2_w_in[l].astype(BF16), ffn2_w_out[l].astype(BF16),
                      final_g=final_norm_g if l == depth - 1 else None)
    return x
```

```python
import functools
import math

import jax
import jax.numpy as jnp
from jax import lax
from jax.experimental import pallas as pl
from jax.experimental.pallas import tpu as pltpu

F32 = jnp.float32
BF16 = jnp.bfloat16

HEAD_DIM = 64
N_HEADS = 6
BRANCH_W = N_HEADS * HEAD_DIM
N_BRANCHES = 3
DECAY_LORA = 64
AAA_LORA = 64
GATE_LORA = 160
RWKV_GN_EPS = 64e-5
MLA_Q_LORA = 256
MLA_KV_LORA = 128
MLA_NOPE_DIM = 64
MLA_ROPE_DIM = 32
MLA_V_DIM = 64
ROPE_THETA = 10000.0
MACARON_WEIGHT = 0.5
NORM_EPS = 1e-6
N_ADA = 9
RWKV_COLS = 3 * BRANCH_W + DECAY_LORA + AAA_LORA + GATE_LORA

LANES = 128
SUBLANES = 8
MXU_WIDTH = 256
VMEM_LIMIT_BYTES = 56 * 1024 * 1024

FFN_ROWS = 512
MERGE_ROWS = 512
ATT_TILE = 256
SCAN_STEPS = 32
SCAN_PARTIALS = 2

RWKV_PAD = 1536
GATE_LORA_PAD = RWKV_PAD - (3 * BRANCH_W + DECAY_LORA + AAA_LORA)
NEG_BIG = -1e30
LOG2E = math.log2(math.e)


def _tile(n, pref):
    t = min(n, pref)
    assert n % t == 0, (n, t)
    return t


def _params(sem):
    return pltpu.CompilerParams(dimension_semantics=sem, vmem_limit_bytes=VMEM_LIMIT_BYTES)


def _const_spec(shape):
    nd = len(shape)
    return pl.BlockSpec(shape, lambda *_: (0,) * nd)


def _resident_spec(shape):
    nd = len(shape)
    return pl.BlockSpec(shape, lambda *_: (0,) * nd, pipeline_mode=pl.Buffered(1))


def _mod_spec(l, j, d):
    return pl.BlockSpec((None, None, None, 1, d), lambda b, *_: (l, j, b, 0, 0))


def _modulated_norm(x, g, shift, scale):
    ms = jnp.mean(x * x, axis=-1, keepdims=True)
    y = x * lax.rsqrt(ms + NORM_EPS) * g
    return y * (1.0 + scale) + shift


def _rms(x, g):
    ms = jnp.mean(x * x, axis=-1, keepdims=True)
    return x * lax.rsqrt(ms + NORM_EPS) * g


def _dot(a, b):
    return jnp.dot(a, b, preferred_element_type=F32)


def _ada_kernel(c_ref, w_ref, b_ref, o_ref):
    c = c_ref[...]
    ca = (c * jax.nn.sigmoid(c)).astype(BF16)
    o_ref[...] = _dot(ca, w_ref[...].astype(BF16)) + b_ref[...]


def _ada_call(c, ada_w, ada_b):
    depth, d, _ = ada_w.shape
    bsz = c.shape[0]
    ada_b4 = ada_b.reshape(depth, N_ADA, 1, d)
    out = pl.pallas_call(
        _ada_kernel,
        out_shape=jax.ShapeDtypeStruct((depth, N_ADA, bsz, d), F32),
        grid=(depth, N_ADA),
        in_specs=[
            pl.BlockSpec((bsz, d), lambda l, j: (0, 0)),
            pl.BlockSpec((None, d, d), lambda l, j: (l, 0, j)),
            pl.BlockSpec((None, None, 1, d), lambda l, j: (l, j, 0, 0)),
        ],
        out_specs=pl.BlockSpec((None, None, bsz, d), lambda l, j: (l, j, 0, 0)),
        compiler_params=_params(("parallel", "parallel")),
        name="ada_mod",
    )(c, ada_w, ada_b4)
    return out.reshape(depth, N_ADA, bsz, 1, d)


def _rope_kernel(pos_ref, freq_ref, cos_ref, sin_ref):
    ang = pos_ref[...].astype(F32) * freq_ref[...]
    lane = lax.broadcasted_iota(jnp.int32, ang.shape, 1)
    in_rope = (lane >= MLA_NOPE_DIM) & (lane < MLA_NOPE_DIM + MLA_ROPE_DIM)
    cos_ref[...] = jnp.where(lane < MLA_NOPE_DIM, 1.0, jnp.where(in_rope, jnp.cos(ang), 0.0))
    sin_ref[...] = jnp.where(in_rope, jnp.sin(ang), 0.0)


def _rope_call(positions):
    bsz, seq = positions.shape
    half = MLA_ROPE_DIM // 2
    inv_freq = 1.0 / (ROPE_THETA ** (jnp.arange(0, MLA_ROPE_DIM, 2, dtype=F32) / MLA_ROPE_DIM))
    freq = jnp.zeros((1, LANES), F32)
    freq = freq.at[0, MLA_NOPE_DIM:MLA_NOPE_DIM + half].set(inv_freq)
    freq = freq.at[0, MLA_NOPE_DIM + half:MLA_NOPE_DIM + 2 * half].set(inv_freq)
    ts = _tile(seq, 512)
    spec = pl.BlockSpec((None, ts, LANES), lambda b, i: (b, i, 0))
    return pl.pallas_call(
        _rope_kernel,
        out_shape=(jax.ShapeDtypeStruct((bsz, seq, LANES), F32),) * 2,
        grid=(bsz, seq // ts),
        in_specs=[pl.BlockSpec((None, ts, 1), lambda b, i: (b, i, 0)), _const_spec((1, LANES))],
        out_specs=(spec, spec),
        compiler_params=_params(("parallel", "parallel")),
        name="rope_tables",
    )(positions.reshape(bsz, seq, 1), freq)


def _ffn_kernel(x_ref, sh_ref, sc_ref, gt_ref, g_ref, wi_ref, wo_ref, *rest, final):
    if final:
        fg_ref, o_ref = rest
    else:
        (o_ref,) = rest
    d_ff = wo_ref.shape[0]
    x = x_ref[...]
    h = _modulated_norm(x, g_ref[...], sh_ref[...], sc_ref[...]).astype(BF16)
    acc = None
    for c0 in range(0, d_ff, MXU_WIDTH):
        gate = _dot(h, wi_ref[:, c0:c0 + MXU_WIDTH])
        up = _dot(h, wi_ref[:, d_ff + c0:d_ff + c0 + MXU_WIDTH])
        act = (gate * jax.nn.sigmoid(gate) * up).astype(BF16)
        part = _dot(act, wo_ref[c0:c0 + MXU_WIDTH, :])
        acc = part if acc is None else acc + part
    out = x + (MACARON_WEIGHT * gt_ref[...]) * acc
    if final:
        out = _rms(out, fg_ref[...])
    o_ref[...] = out


def _ffn_call(x, mod5, l, sub, norm_g, w_in, w_out, final_g=None):
    bsz, seq, d = x.shape
    d_ff = w_out.shape[0]
    assert d_ff % MXU_WIDTH == 0
    tm = _tile(seq, FFN_ROWS)
    final = final_g is not None
    xspec = pl.BlockSpec((None, tm, d), lambda b, i: (b, i, 0))
    in_specs = [
        xspec,
        _mod_spec(l, 3 * sub + 0, d), _mod_spec(l, 3 * sub + 1, d), _mod_spec(l, 3 * sub + 2, d),
        _const_spec((1, d)),
        _resident_spec(w_in.shape), _resident_spec(w_out.shape),
    ]
    args = [x, mod5, mod5, mod5, norm_g.reshape(1, d), w_in, w_out]
    if final:
        in_specs.append(_const_spec((1, d)))
        args.append(final_g.reshape(1, d))
    return pl.pallas_call(
        functools.partial(_ffn_kernel, final=final),
        out_shape=jax.ShapeDtypeStruct((bsz, seq, d), F32),
        grid=(bsz, seq // tm),
        in_specs=in_specs,
        out_specs=xspec,
        compiler_params=_params(("parallel", "parallel")),
        name="ffn",
    )(*args)


def _prep_kernel(x_ref, sh_ref, sc_ref, g_ref, wa_ref, wb_ref, mu_ref, w0_ref, a0_ref,
                 w2a2_ref, g2_ref, qg_ref, kvg_ref, wq_ref, wk_ref, wv_ref, cos_ref, sin_ref,
                 r_ref, k_ref, v_ref, w_ref, a_ref, gr_ref,
                 sqt_ref, sk_ref, svt_ref, mqt_ref, mk_ref, mvt_ref,
                 carry_ref):
    i = pl.program_id(1)
    bw = BRANCH_W

    @pl.when(i == 0)
    def _():
        carry_ref[...] = jnp.zeros_like(carry_ref)

    h = _modulated_norm(x_ref[...], g_ref[...], sh_ref[...], sc_ref[...]).astype(BF16)
    tm = h.shape[0]

    p = _dot(h, wa_ref[...])
    row = lax.broadcasted_iota(jnp.int32, p.shape, 0)
    prev = jnp.where(row == 0, carry_ref[...], pltpu.roll(p, 1, 0))
    carry_ref[...] = p[tm - 1:tm, :]
    ps = p + (prev - p) * mu_ref[...]
    r_ref[...] = ps[:, 0:bw]
    k_ref[...] = ps[:, bw:2 * bw]
    v_ref[...] = ps[:, 2 * bw:3 * bw]
    lora = ps[:, 3 * bw:3 * bw + LANES]
    lane = lax.broadcasted_iota(jnp.int32, lora.shape, 1)
    lora = jnp.where(lane < DECAY_LORA, jnp.tanh(lora), lora).astype(BF16)
    wa = _dot(lora, w2a2_ref[...])
    w_log = -jax.nn.softplus(-(w0_ref[...] + wa[:, :bw])) - 0.5
    w_ref[...] = jnp.exp(w_log)
    a_ref[...] = jax.nn.sigmoid(a0_ref[...] + wa[:, bw:])
    dg = jax.nn.sigmoid(ps[:, 3 * bw + LANES:]).astype(BF16)
    gr_ref[...] = _dot(dg, g2_ref[...])

    pb = _dot(h, wb_ref[...])
    sqt_ref[...] = (pb[:, 0:bw] * (HEAD_DIM ** -0.5 * LOG2E)).T.astype(BF16)
    sk_ref[...] = pb[:, bw:2 * bw].astype(BF16)
    svt_ref[...] = pb[:, 2 * bw:3 * bw].T.astype(BF16)
    off = 3 * bw
    cq = _rms(pb[:, off:off + MLA_Q_LORA], qg_ref[...]).astype(BF16)
    off += MLA_Q_LORA
    ckv = _rms(pb[:, off:off + MLA_KV_LORA], kvg_ref[...]).astype(BF16)
    off += MLA_KV_LORA
    cos = cos_ref[...]
    sin = sin_ref[...]
    kr = pb[:, off:off + LANES] * cos + pb[:, off + LANES:off + 2 * LANES] * sin
    qall = _dot(cq, wq_ref[...])
    kall = _dot(ckv, wk_ref[...])
    mvt_ref[...] = _dot(ckv, wv_ref[...]).T.astype(BF16)
    scale = (MLA_NOPE_DIM + MLA_ROPE_DIM) ** -0.5 * LOG2E
    for hd in range(N_HEADS):
        q = qall[:, 2 * hd * LANES:(2 * hd + 1) * LANES] * cos + qall[:, (2 * hd + 1) * LANES:(2 * hd + 2) * LANES] * sin
        mqt_ref[hd * LANES:(hd + 1) * LANES, :] = (q * scale).T.astype(BF16)
        mk_ref[hd] = (kall[:, hd * LANES:(hd + 1) * LANES] + kr).astype(BF16)


def _rotate_half_cols(w):
    half = MLA_ROPE_DIM // 2
    return jnp.concatenate([-w[..., half:], w[..., :half]], axis=-1)


def _prep_weights(mix_w_in, rwkv_mu, rwkv_w2, rwkv_a2, rwkv_g2, mla_w_uq, mla_w_ukv):
    d = mix_w_in.shape[0]
    bw = BRANCH_W
    wa = jnp.pad(mix_w_in[:, :RWKV_COLS], ((0, 0), (0, RWKV_PAD - RWKV_COLS))).astype(BF16)
    mu = jnp.pad(rwkv_mu, (0, RWKV_PAD - RWKV_COLS)).reshape(1, RWKV_PAD)
    o = RWKV_COLS
    qkv_cq_ckv = mix_w_in[:, o:o + 3 * bw + MLA_Q_LORA + MLA_KV_LORA]
    o += 3 * bw + MLA_Q_LORA + MLA_KV_LORA
    w_kr = mix_w_in[:, o:o + MLA_ROPE_DIM]
    o += MLA_ROPE_DIM
    z_lo = jnp.zeros((d, MLA_NOPE_DIM), F32)
    z_hi = jnp.zeros((d, LANES - MLA_NOPE_DIM - MLA_ROPE_DIM), F32)
    wb = jnp.concatenate([qkv_cq_ckv, z_lo, w_kr, z_hi, z_lo, _rotate_half_cols(w_kr), z_hi], axis=1).astype(BF16)
    w_gate = mix_w_in[:, o:].astype(BF16)
    w2a2 = jnp.zeros((LANES, 2 * bw), F32)
    w2a2 = w2a2.at[:DECAY_LORA, :bw].set(rwkv_w2).at[DECAY_LORA:, bw:].set(rwkv_a2).astype(BF16)
    g2 = jnp.pad(rwkv_g2, ((0, GATE_LORA_PAD - GATE_LORA), (0, 0))).astype(BF16)
    qd = MLA_NOPE_DIM + MLA_ROPE_DIM
    uq = mla_w_uq.reshape(MLA_Q_LORA, N_HEADS, qd)
    zq = jnp.zeros((MLA_Q_LORA, N_HEADS, LANES - qd), F32)
    zn = jnp.zeros((MLA_Q_LORA, N_HEADS, MLA_NOPE_DIM), F32)
    wq = jnp.concatenate([uq, zq, zn, _rotate_half_cols(uq[..., MLA_NOPE_DIM:]), zq], axis=-1)
    wq = wq.reshape(MLA_Q_LORA, N_HEADS * 2 * LANES).astype(BF16)
    ukv = mla_w_ukv.reshape(MLA_KV_LORA, N_HEADS, MLA_NOPE_DIM + MLA_V_DIM)
    wk = jnp.concatenate([ukv[..., :MLA_NOPE_DIM], jnp.zeros((MLA_KV_LORA, N_HEADS, LANES - MLA_NOPE_DIM), F32)], axis=-1)
    wk = wk.reshape(MLA_KV_LORA, N_HEADS * LANES).astype(BF16)
    wv = ukv[..., MLA_NOPE_DIM:].reshape(MLA_KV_LORA, N_HEADS * MLA_V_DIM).astype(BF16)
    return wa, wb, mu, w2a2, g2, wq, wk, wv, w_gate


def _prep_call(x, mod5, l, norm_g, pw, w0, a0, q_norm_g, kv_norm_g, cos_t, sin_t):
    bsz, seq, d = x.shape
    wa, wb, mu, w2a2, g2, wq, wk, wv, _ = pw
    bw = BRANCH_W
    tm = _tile(seq, ATT_TILE)
    nt = seq // tm
    row = lambda b, i: (b, i, 0)
    col = lambda b, i: (b, 0, i)
    xspec = pl.BlockSpec((None, tm, d), row)
    bspec = pl.BlockSpec((None, tm, bw), row)
    lspec = pl.BlockSpec((None, tm, LANES), row)
    f32_out = jax.ShapeDtypeStruct((bsz, seq, bw), F32)
    out_shape = (f32_out,) * 6 + (
        jax.ShapeDtypeStruct((bsz, bw, seq), BF16),
        jax.ShapeDtypeStruct((bsz, seq, bw), BF16),
        jax.ShapeDtypeStruct((bsz, nt, bw, tm), BF16),
        jax.ShapeDtypeStruct((bsz, N_HEADS * LANES, seq), BF16),
        jax.ShapeDtypeStruct((bsz, N_HEADS, seq, LANES), BF16),
        jax.ShapeDtypeStruct((bsz, nt, bw, tm), BF16),
    )
    vt_spec = pl.BlockSpec((None, None, bw, tm), lambda b, i: (b, i, 0, 0))
    out_specs = (bspec,) * 6 + (
        pl.BlockSpec((None, bw, tm), col),
        bspec,
        vt_spec,
        pl.BlockSpec((None, N_HEADS * LANES, tm), col),
        pl.BlockSpec((None, N_HEADS, tm, LANES), lambda b, i: (b, 0, i, 0)),
        vt_spec,
    )
    return pl.pallas_call(
        _prep_kernel,
        out_shape=out_shape,
        grid=(bsz, nt),
        in_specs=[
            xspec, _mod_spec(l, 3, d), _mod_spec(l, 4, d), _const_spec((1, d)),
            _resident_spec(wa.shape), _resident_spec(wb.shape), _const_spec((1, RWKV_PAD)),
            _const_spec((1, bw)), _const_spec((1, bw)),
            _resident_spec(w2a2.shape), _resident_spec(g2.shape),
            _const_spec((1, MLA_Q_LORA)), _const_spec((1, MLA_KV_LORA)),
            _resident_spec(wq.shape), _resident_spec(wk.shape), _resident_spec(wv.shape),
            lspec, lspec,
        ],
        out_specs=out_specs,
        scratch_shapes=[pltpu.VMEM((1, RWKV_PAD), F32)],
        compiler_params=_params(("parallel", "arbitrary")),
        name="mixer_prep",
    )(x, mod5, mod5, norm_g.reshape(1, d), wa, wb, mu, w0.reshape(1, bw), a0.reshape(1, bw),
      w2a2, g2, q_norm_g.reshape(1, -1), kv_norm_g.reshape(1, -1), wq, wk, wv, cos_t, sin_t)


def _sb_kernel(qt_ref, k_ref, vt_ref, tri_ref, o_ref, acc_ref, c_ref):
    qi = pl.program_id(1)
    tk = vt_ref.shape[2]
    tq = qt_ref.shape[1]
    heads = range(N_HEADS)
    frow = lax.broadcasted_iota(jnp.int32, (LANES, tq), 0)
    q_heads = []
    for hd in heads:
        qt = qt_ref[(hd // 2) * LANES:(hd // 2 + 1) * LANES, :]
        keep = frow < HEAD_DIM if hd % 2 == 0 else frow >= HEAD_DIM
        q_heads.append(jnp.where(keep, qt, jnp.zeros_like(qt)))
    acc_ref[...] = jnp.zeros_like(acc_ref)
    c_ref[...] = jnp.zeros_like(c_ref)

    def tiles(js, diag):
        tri = tri_ref[...]
        if diag:
            key = lax.broadcasted_iota(jnp.int32, (tk, tq), 0)
            qry = lax.broadcasted_iota(jnp.int32, (tk, tq), 1)
            mask = key < qry
        work = [(ji, hd) for ji in range(len(js)) for hd in heads]
        ks = [k_ref[pl.ds(pl.multiple_of(j * tk, tk), tk), :] for j in js]
        zs = [_dot(ks[ji][:, (hd // 2) * LANES:(hd // 2 + 1) * LANES], q_heads[hd]) for ji, hd in work]
        sps, lss = [], []
        for z in zs:
            neg_abs = lax.bitcast_convert_type(lax.bitcast_convert_type(z, jnp.uint32) | jnp.uint32(0x80000000), F32)
            ls = jnp.minimum(z, 0.0) - jnp.log2(1.0 + jnp.exp2(neg_abs))
            sp = z - ls
            if diag:
                sp = jnp.where(mask, sp, 0.0)
            sps.append(sp)
            lss.append(ls)
        afters = []
        for sp in sps:
            hi = sp.astype(BF16)
            lo = (sp - hi.astype(F32)).astype(BF16)
            afters.append(_dot(tri, jnp.concatenate([hi, lo], axis=0)))
        cs = [c_ref[hd] for hd in heads]
        ws = []
        for i, (ji, hd) in enumerate(work):
            w = jnp.exp2(lss[i] - afters[i][:tk] - cs[hd])
            if diag:
                w = jnp.where(mask, w, 0.0)
            ws.append(w.astype(BF16))
            cs[hd] = cs[hd] + afters[i][tk:tk + 1]
        for hd in heads:
            c_ref[hd] = cs[hd]
        vts = [vt_ref[j] for j in js]
        for hd in heads:
            rows = slice(hd * HEAD_DIM, (hd + 1) * HEAD_DIM)
            upd = None
            for i, (ji, h2) in enumerate(work):
                if h2 == hd:
                    part = _dot(vts[ji][rows, :], ws[i])
                    upd = part if upd is None else upd + part
            acc_ref[rows, :] += upd

    tiles([qi], True)

    def body(jj, carry):
        j = qi - 1 - 2 * jj
        tiles([j, j - 1], False)
        return carry

    lax.fori_loop(0, qi // 2, body, 0)

    @pl.when(qi % 2 == 1)
    def _():
        tiles([0], False)

    o_ref[...] = acc_ref[...].T.astype(o_ref.dtype)


def _sb_call(qt, k, vt):
    bsz, seq, bw = k.shape
    nk, tk = vt.shape[1], vt.shape[3]
    tq = tk
    tri = lax.broadcasted_iota(jnp.int32, (tk, tk), 1) > lax.broadcasted_iota(jnp.int32, (tk, tk), 0)
    tri = jnp.concatenate([tri, jnp.ones((SUBLANES, tk), jnp.bool_)], axis=0).astype(BF16)
    tri = jnp.concatenate([tri, tri], axis=1)
    return pl.pallas_call(
        _sb_kernel,
        out_shape=jax.ShapeDtypeStruct(k.shape, BF16),
        grid=(bsz, seq // tq),
        in_specs=[
            pl.BlockSpec((None, bw, tq), lambda b, i: (b, 0, i)),
            pl.BlockSpec((None, seq, bw), lambda b, i: (b, 0, 0)),
            pl.BlockSpec((None, nk, bw, tk), lambda b, i: (b, 0, 0, 0)),
            _const_spec(tri.shape),
        ],
        out_specs=pl.BlockSpec((None, tq, bw), lambda b, i: (b, i, 0)),
        scratch_shapes=[pltpu.VMEM((bw, tq), F32), pltpu.VMEM((N_HEADS, 1, tq), F32)],
        compiler_params=_params(("parallel", "parallel")),
        name="stick_breaking",
    )(qt, k, vt, tri)


def _mla_kernel(qt_ref, k_ref, vt_ref, o_ref, acc_ref, m_ref, l_ref):
    qi = pl.program_id(1)
    tk = vt_ref.shape[2]
    tq = qt_ref.shape[1]
    heads = range(N_HEADS)
    acc_ref[...] = jnp.zeros_like(acc_ref)
    m_ref[...] = jnp.full_like(m_ref, NEG_BIG)
    l_ref[...] = jnp.zeros_like(l_ref)

    def tiles(j, nt, diag):
        start = pl.multiple_of(j * tk, tk)
        vt = jnp.concatenate([vt_ref[j + i] for i in range(nt)], axis=1) if nt > 1 else vt_ref[j]
        if diag:
            key = lax.broadcasted_iota(jnp.int32, (tk, tq), 0)
            qry = lax.broadcasted_iota(jnp.int32, (tk, tq), 1)
            mask = key <= qry
        ss = [_dot(k_ref[hd, pl.ds(start, nt * tk), :], qt_ref[hd * LANES:(hd + 1) * LANES, :])
              for hd in heads]
        if diag:
            ss = [jnp.where(mask, s, NEG_BIG) for s in ss]
        ps, alphas = [], []
        for hd in heads:
            m_old = m_ref[hd]
            m_new = jnp.maximum(m_old, jnp.max(ss[hd], axis=0, keepdims=True))
            p = jnp.exp2(ss[hd] - m_new)
            alpha = jnp.exp2(m_old - m_new)
            l_ref[hd] = alpha * l_ref[hd] + jnp.sum(p, axis=0, keepdims=True)
            m_ref[hd] = m_new
            ps.append(p.astype(BF16))
            alphas.append(alpha)
        for hd in heads:
            rows = slice(hd * HEAD_DIM, (hd + 1) * HEAD_DIM)
            acc_ref[rows, :] = alphas[hd] * acc_ref[rows, :] + _dot(vt[rows, :], ps[hd])

    tiles(qi, 1, True)

    def body(jj, carry):
        tiles(2 * jj, 2, False)
        return carry

    lax.fori_loop(0, qi // 2, body, 0)

    @pl.when(qi % 2 == 1)
    def _():
        tiles(qi - 1, 1, False)

    out = jnp.concatenate(
        [acc_ref[hd * HEAD_DIM:(hd + 1) * HEAD_DIM, :] / l_ref[hd] for hd in heads], axis=0)
    o_ref[...] = out.T.astype(o_ref.dtype)


def _mla_call(qt, k, vt):
    bsz, _, seq, _ = k.shape
    nk, bw, tk = vt.shape[1:]
    tq = tk
    return pl.pallas_call(
        _mla_kernel,
        out_shape=jax.ShapeDtypeStruct((bsz, seq, bw), BF16),
        grid=(bsz, seq // tq),
        in_specs=[
            pl.BlockSpec((None, N_HEADS * LANES, tq), lambda b, i: (b, 0, i)),
            pl.BlockSpec((None, N_HEADS, seq, LANES), lambda b, i: (b, 0, 0, 0)),
            pl.BlockSpec((None, nk, bw, tk), lambda b, i: (b, 0, 0, 0)),
        ],
        out_specs=pl.BlockSpec((None, tq, bw), lambda b, i: (b, i, 0)),
        scratch_shapes=[pltpu.VMEM((bw, tq), F32), pltpu.VMEM((N_HEADS, 1, tq), F32),
                        pltpu.VMEM((N_HEADS, 1, tq), F32)],
        compiler_params=_params(("parallel", "parallel")),
        name="latent_attention",
    )(qt, k, vt)


def _scan_kernel(r_ref, e_ref, k_ref, v_ref, a_ref, kk_ref, ka_ref, rk_ref, lg_ref, lb_ref,
                 o_ref, h_ref, vec_ref, cum_ref):
    n = HEAD_DIM

    @pl.when(pl.program_id(0) == 0)
    def _():
        h_ref[...] = jnp.zeros_like(h_ref)

    cum_ref[...] = jnp.zeros_like(cum_ref)

    def bcast_row(ref, c):
        return ref[pl.ds(c, 1), :]

    def step(t, carry):
        kt = k_ref[t]
        at = a_ref[t]
        rt = r_ref[t]
        vt = v_ref[t]
        e_prev = cum_ref[...]
        e_cum = e_prev + e_ref[t]
        cum_ref[...] = e_cum
        g_inv = jnp.exp(e_cum)
        kk = kt * kk_ref[...]
        kk = kk * lax.rsqrt(jnp.maximum(jnp.sum(kk * kk, axis=0, keepdims=True), 1e-24))
        km = kt * (1.0 + (at - 1.0) * ka_ref[...])
        vec_ref[0] = kk * jnp.exp(-e_prev)
        vec_ref[1] = kk * at * g_inv
        vec_ref[2] = km * g_inv
        vec_ref[3] = rt * jnp.exp(-e_cum)
        bonus = jnp.sum(rt * km * rk_ref[...], axis=0, keepdims=True)
        sa_parts = [jnp.zeros_like(vt) for _ in range(SCAN_PARTIALS)]
        for c in range(n):
            sa_parts[c % SCAN_PARTIALS] = sa_parts[c % SCAN_PARTIALS] - h_ref[c] * bcast_row(vec_ref.at[0], c)
        sa = functools.reduce(lambda p, q: p + q, sa_parts)
        y_parts = [jnp.zeros_like(vt) for _ in range(SCAN_PARTIALS)]
        for c in range(n):
            hn = h_ref[c] + sa * bcast_row(vec_ref.at[1], c) + vt * bcast_row(vec_ref.at[2], c)
            h_ref[c] = hn
            y_parts[c % SCAN_PARTIALS] = y_parts[c % SCAN_PARTIALS] + hn * bcast_row(vec_ref.at[3], c)
        y = functools.reduce(lambda p, q: p + q, y_parts)
        mean = jnp.mean(y, axis=(0, 1), keepdims=True)
        yc = y - mean
        var = jnp.mean(yc * yc, axis=(0, 1), keepdims=True)
        yn = yc * lax.rsqrt(var + RWKV_GN_EPS) * lg_ref[...] + lb_ref[...]
        o_ref[t] = yn + bonus * vt
        return carry

    lax.fori_loop(0, r_ref.shape[0], step, 0)
    vec_ref[0] = jnp.exp(-cum_ref[...])
    for c in range(n):
        h_ref[c] = h_ref[c] * bcast_row(vec_ref.at[0], c)


def _to_chains(t):
    bsz, seq, _ = t.shape
    return t.reshape(bsz, seq, N_HEADS, HEAD_DIM).transpose(1, 3, 0, 2).reshape(seq, HEAD_DIM, bsz * N_HEADS)


def _param_chains(p, bsz):
    return jnp.tile(p.reshape(N_HEADS, HEAD_DIM).T, (1, bsz))


def _scan_call(r, w, k, v, a, k_k, k_a, r_k, lnx_g, lnx_b):
    bsz, seq, _ = r.shape
    chains = bsz * N_HEADS
    n = HEAD_DIM
    sub = n // SUBLANES
    r, w, k, v, a = (_to_chains(t) for t in (r, w, k, v, a))
    v = v.reshape(seq, sub, SUBLANES, chains)
    k_k, k_a, r_k, lnx_g, lnx_b = (_param_chains(p.reshape(-1), bsz) for p in (k_k, k_a, r_k, lnx_g, lnx_b))
    lnx_g = lnx_g.reshape(sub, SUBLANES, chains)
    lnx_b = lnx_b.reshape(sub, SUBLANES, chains)
    ts = _tile(seq, SCAN_STEPS)
    tspec = pl.BlockSpec((ts, n, chains), lambda i: (i, 0, 0))
    vspec = pl.BlockSpec((ts, sub, SUBLANES, chains), lambda i: (i, 0, 0, 0))
    pspec = _const_spec((n, chains))
    gspec = _const_spec((sub, SUBLANES, chains))
    out = pl.pallas_call(
        _scan_kernel,
        out_shape=jax.ShapeDtypeStruct((seq, sub, SUBLANES, chains), F32),
        grid=(seq // ts,),
        in_specs=[tspec, tspec, tspec, vspec, tspec, pspec, pspec, pspec, gspec, gspec],
        out_specs=vspec,
        scratch_shapes=[pltpu.VMEM((n, sub, SUBLANES, chains), F32), pltpu.VMEM((4, n, chains), F32),
                        pltpu.VMEM((n, chains), F32)],
        compiler_params=_params(("arbitrary",)),
        name="rwkv7_scan",
    )(r, w, k, v, a, k_k, k_a, r_k, lnx_g, lnx_b)
    out = out.reshape(seq, n, bsz, N_HEADS).transpose(2, 0, 3, 1)
    return out.reshape(bsz, seq, BRANCH_W)


def _merge_kernel(x_ref, sh_ref, sc_ref, gt_ref, g_ref, wg_ref, ya_ref, gr_ref, yb_ref, yc_ref,
                  bw_ref, wo_ref, o_ref):
    x = x_ref[...]
    d = x.shape[1]
    h = _modulated_norm(x, g_ref[...], sh_ref[...], sc_ref[...]).astype(BF16)
    ys = ((ya_ref[...] * gr_ref[...]).astype(BF16), yb_ref[...], yc_ref[...])
    merged = None
    for n in range(N_BRANCHES):
        gate = jax.nn.sigmoid(_dot(h, wg_ref[:, n * d:(n + 1) * d]))
        term = gate * _dot(ys[n], bw_ref[n])
        merged = term if merged is None else merged + term
    o_ref[...] = x + gt_ref[...] * _dot(merged.astype(BF16), wo_ref[...])


def _merge_call(x, mod5, l, norm_g, w_gate, ya, gr, yb, yc, branch_w, w_out):
    bsz, seq, d = x.shape
    bw = BRANCH_W
    tm = _tile(seq, MERGE_ROWS)
    row = lambda b, i: (b, i, 0)
    xspec = pl.BlockSpec((None, tm, d), row)
    bspec = pl.BlockSpec((None, tm, bw), row)
    return pl.pallas_call(
        _merge_kernel,
        out_shape=jax.ShapeDtypeStruct(x.shape, F32),
        grid=(bsz, seq // tm),
        in_specs=[
            xspec, _mod_spec(l, 3, d), _mod_spec(l, 4, d), _mod_spec(l, 5, d), _const_spec((1, d)),
            _resident_spec(w_gate.shape), bspec, bspec, bspec, bspec,
            _resident_spec(branch_w.shape), _resident_spec(w_out.shape),
        ],
        out_specs=xspec,
        compiler_params=_params(("parallel", "parallel")),
        name="mixer_merge",
    )(x, mod5, mod5, mod5, norm_g.reshape(1, d), w_gate, ya, gr, yb, yc, branch_w, w_out)


def kernel(x, c, positions, ada_w, ada_b, norm_g, ffn1_w_in, ffn1_w_out, mix_w_in, rwkv_mu, rwkv_w0, rwkv_w2, rwkv_a0, rwkv_a2, rwkv_g2, rwkv_k_k, rwkv_k_a, rwkv_r_k, rwkv_lnx_g, rwkv_lnx_b, mla_q_norm_g, mla_w_uq, mla_kv_norm_g, mla_w_ukv, branch_w, mix_w_out, ffn2_w_in, ffn2_w_out, final_norm_g):
    depth = ada_w.shape[0]
    mod5 = _ada_call(c, ada_w, ada_b)
    cos_t, sin_t = _rope_call(positions)
    for l in range(depth):
        x = _ffn_call(x, mod5, l, 0, norm_g[l, 0], ffn1_w_in[l].astype(BF16), ffn1_w_out[l].astype(BF16))
        pw = _prep_weights(mix_w_in[l], rwkv_mu[l], rwkv_w2[l], rwkv_a2[l], rwkv_g2[l], mla_w_uq[l], mla_w_ukv[l])
        (r, k, v, w, a, gr, sqt, sk, svt, mqt, mk, mvt) = _prep_call(
            x, mod5, l, norm_g[l, 1], pw, rwkv_w0[l], rwkv_a0[l], mla_q_norm_g[l], mla_kv_norm_g[l], cos_t, sin_t)
        ya = _scan_call(r, w, k, v, a, rwkv_k_k[l], rwkv_k_a[l], rwkv_r_k[l], rwkv_lnx_g[l], rwkv_lnx_b[l])
        yb = _sb_call(sqt, sk, svt)
        yc = _mla_call(mqt, mk, mvt)
        x = _merge_call(x, mod5, l, norm_g[l, 1], pw[-1], ya, gr, yb, yc,
                        branch_w[l].astype(BF16), mix_w_out[l].astype(BF16))
        x = _ffn_call(x, mod5, l, 2, norm_g[l, 2], ffn2_w_in[l].astype(BF16), ffn2_w_out[l].astype(BF16),
                      final_g=final_norm_g if l == depth - 1 else None)
    return x
```

```python
import functools
import math

import jax
import jax.numpy as jnp
from jax import lax
from jax.experimental import pallas as pl
from jax.experimental.pallas import tpu as pltpu

F32 = jnp.float32
BF16 = jnp.bfloat16

HEAD_DIM = 64
N_HEADS = 6
BRANCH_W = N_HEADS * HEAD_DIM
N_BRANCHES = 3
DECAY_LORA = 64
AAA_LORA = 64
GATE_LORA = 160
RWKV_GN_EPS = 64e-5
MLA_Q_LORA = 256
MLA_KV_LORA = 128
MLA_NOPE_DIM = 64
MLA_ROPE_DIM = 32
MLA_V_DIM = 64
ROPE_THETA = 10000.0
MACARON_WEIGHT = 0.5
NORM_EPS = 1e-6
N_ADA = 9
RWKV_COLS = 3 * BRANCH_W + DECAY_LORA + AAA_LORA + GATE_LORA

LANES = 128
SUBLANES = 8
MXU_WIDTH = 256
VMEM_LIMIT_BYTES = 56 * 1024 * 1024

FFN_ROWS = 512
PREP_ROWS = 512
ATT_TILE = 256
CAST_ROWS = 256
SCAN_STEPS = 32
SCAN_PARTIALS = 2

RWKV_PAD = 1536
GATE_LORA_PAD = RWKV_PAD - (3 * BRANCH_W + DECAY_LORA + AAA_LORA)
NEG_BIG = -1e30
LOG2E = math.log2(math.e)


def _tile(n, pref):
    t = min(n, pref)
    assert n % t == 0, (n, t)
    return t


def _params(sem):
    return pltpu.CompilerParams(dimension_semantics=sem, vmem_limit_bytes=VMEM_LIMIT_BYTES)


def _const_spec(shape):
    nd = len(shape)
    return pl.BlockSpec(shape, lambda *_: (0,) * nd)


def _resident_spec(shape):
    nd = len(shape)
    return pl.BlockSpec(shape, lambda *_: (0,) * nd, pipeline_mode=pl.Buffered(1))


def _layer_spec(shape, *lead):
    n_lead = len(lead)
    tail = tuple(shape[n_lead:])
    return pl.BlockSpec((None,) * n_lead + tail, lambda *_: tuple(lead) + (0,) * len(tail),
                        pipeline_mode=pl.Buffered(1))


def _mod_spec(l, j, d):
    return pl.BlockSpec((None, None, None, 1, d), lambda b, *_: (l, j, b, 0, 0))


def _modulated_norm(x, g, shift, scale):
    ms = jnp.mean(x * x, axis=-1, keepdims=True)
    y = x * lax.rsqrt(ms + NORM_EPS) * g
    return y * (1.0 + scale) + shift


def _rms(x, g):
    ms = jnp.mean(x * x, axis=-1, keepdims=True)
    return x * lax.rsqrt(ms + NORM_EPS) * g


def _dot(a, b):
    return jnp.dot(a, b, preferred_element_type=F32)


def _ada_kernel(c_ref, w_ref, b_ref, o_ref):
    c = c_ref[...]
    ca = (c * jax.nn.sigmoid(c)).astype(BF16)
    o_ref[...] = _dot(ca, w_ref[...].astype(BF16)) + b_ref[...]


def _ada_call(c, ada_w, ada_b):
    depth, d, _ = ada_w.shape
    bsz = c.shape[0]
    ada_b4 = ada_b.reshape(depth, N_ADA, 1, d)
    out = pl.pallas_call(
        _ada_kernel,
        out_shape=jax.ShapeDtypeStruct((depth, N_ADA, bsz, d), F32),
        grid=(depth, N_ADA),
        in_specs=[
            pl.BlockSpec((bsz, d), lambda l, j: (0, 0)),
            pl.BlockSpec((None, d, d), lambda l, j: (l, 0, j)),
            pl.BlockSpec((None, None, 1, d), lambda l, j: (l, j, 0, 0)),
        ],
        out_specs=pl.BlockSpec((None, None, bsz, d), lambda l, j: (l, j, 0, 0)),
        compiler_params=_params(("parallel", "parallel")),
        name="ada_mod",
    )(c, ada_w, ada_b4)
    return out.reshape(depth, N_ADA, bsz, 1, d)


def _rope_kernel(pos_ref, freq_ref, cos_ref, sin_ref):
    ang = pos_ref[...].astype(F32) * freq_ref[...]
    lane = lax.broadcasted_iota(jnp.int32, ang.shape, 1)
    in_rope = (lane >= MLA_NOPE_DIM) & (lane < MLA_NOPE_DIM + MLA_ROPE_DIM)
    cos_ref[...] = jnp.where(lane < MLA_NOPE_DIM, 1.0, jnp.where(in_rope, jnp.cos(ang), 0.0))
    sin_ref[...] = jnp.where(in_rope, jnp.sin(ang), 0.0)


def _rope_call(positions):
    bsz, seq = positions.shape
    half = MLA_ROPE_DIM // 2
    inv_freq = 1.0 / (ROPE_THETA ** (jnp.arange(0, MLA_ROPE_DIM, 2, dtype=F32) / MLA_ROPE_DIM))
    freq = jnp.zeros((1, LANES), F32)
    freq = freq.at[0, MLA_NOPE_DIM:MLA_NOPE_DIM + half].set(inv_freq)
    freq = freq.at[0, MLA_NOPE_DIM + half:MLA_NOPE_DIM + 2 * half].set(inv_freq)
    ts = _tile(seq, 512)
    spec = pl.BlockSpec((None, ts, LANES), lambda b, i: (b, i, 0))
    return pl.pallas_call(
        _rope_kernel,
        out_shape=(jax.ShapeDtypeStruct((bsz, seq, LANES), F32),) * 2,
        grid=(bsz, seq // ts),
        in_specs=[pl.BlockSpec((None, ts, 1), lambda b, i: (b, i, 0)), _const_spec((1, LANES))],
        out_specs=(spec, spec),
        compiler_params=_params(("parallel", "parallel")),
        name="rope_tables",
    )(positions.reshape(bsz, seq, 1), freq)


def _ffn_body(x, shift, scale, gate_mod, g, wi_ref, wo_ref):
    d_ff = wo_ref.shape[0]
    assert d_ff % MXU_WIDTH == 0
    h = _modulated_norm(x, g, shift, scale).astype(BF16)
    acc = None
    for c0 in range(0, d_ff, MXU_WIDTH):
        gate = _dot(h, wi_ref[:, c0:c0 + MXU_WIDTH])
        up = _dot(h, wi_ref[:, d_ff + c0:d_ff + c0 + MXU_WIDTH])
        act = (gate * jax.nn.sigmoid(gate) * up).astype(BF16)
        part = _dot(act, wo_ref[c0:c0 + MXU_WIDTH, :])
        acc = part if acc is None else acc + part
    return x + (MACARON_WEIGHT * gate_mod) * acc


def _ffn_kernel(x_ref, sh_ref, sc_ref, gt_ref, g_ref, wi_ref, wo_ref, o_ref):
    o_ref[...] = _ffn_body(x_ref[...], sh_ref[...], sc_ref[...], gt_ref[...], g_ref[...], wi_ref, wo_ref)


def _ffn_call(x, mod5, l, norm_g4, w_in, w_out):
    bsz, seq, d = x.shape
    tm = _tile(seq, FFN_ROWS)
    xspec = pl.BlockSpec((None, tm, d), lambda b, i: (b, i, 0))
    return pl.pallas_call(
        _ffn_kernel,
        out_shape=jax.ShapeDtypeStruct((bsz, seq, d), F32),
        grid=(bsz, seq // tm),
        in_specs=[
            xspec, _mod_spec(l, 0, d), _mod_spec(l, 1, d), _mod_spec(l, 2, d),
            _layer_spec(norm_g4.shape, l, 0), _layer_spec(w_in.shape, l), _layer_spec(w_out.shape, l),
        ],
        out_specs=xspec,
        compiler_params=_params(("parallel", "parallel")),
        name="ffn",
    )(x, mod5, mod5, mod5, norm_g4, w_in, w_out)


def _cast_kernel(x_ref, o_ref):
    o_ref[...] = x_ref[...].astype(o_ref.dtype)


def _cast_call(w, rows=CAST_ROWS):
    depth, r, c = w.shape
    tr = _tile(r, rows)
    spec = pl.BlockSpec((None, tr, c), lambda l, i: (l, i, 0))
    return pl.pallas_call(
        _cast_kernel,
        out_shape=jax.ShapeDtypeStruct(w.shape, BF16),
        grid=(depth, r // tr),
        in_specs=[spec],
        out_specs=spec,
        compiler_params=_params(("parallel", "parallel")),
        name="cast_bf16",
    )(w)


def _store_key_tiles(vt_ref, vt):
    tk = vt_ref.shape[2]
    for j in range(vt_ref.shape[0]):
        vt_ref[j] = vt[:, j * tk:(j + 1) * tk]


def _prep_kernel(x_ref, sh_ref, sc_ref, g_ref, wa_ref, wb_ref, mu_ref, w0_ref, a0_ref,
                 w2a2_ref, g2_ref, qg_ref, kvg_ref, wq_ref, wk_ref, wv_ref, cos_ref, sin_ref,
                 r_ref, k_ref, v_ref, w_ref, a_ref, gr_ref,
                 sqt_ref, sk_ref, svt_ref, mqt_ref, mk_ref, mvt_ref,
                 carry_ref):
    i = pl.program_id(1)
    bw = BRANCH_W

    @pl.when(i == 0)
    def _():
        carry_ref[...] = jnp.zeros_like(carry_ref)

    h = _modulated_norm(x_ref[...], g_ref[...], sh_ref[...], sc_ref[...]).astype(BF16)
    tm = h.shape[0]

    p = _dot(h, wa_ref[...])
    row = lax.broadcasted_iota(jnp.int32, p.shape, 0)
    prev = jnp.where(row == 0, carry_ref[...], pltpu.roll(p, 1, 0))
    carry_ref[...] = p[tm - 1:tm, :]
    ps = p + (prev - p) * mu_ref[...]
    r_ref[...] = ps[:, 0:bw]
    k_ref[...] = ps[:, bw:2 * bw]
    v_ref[...] = ps[:, 2 * bw:3 * bw]
    lora = ps[:, 3 * bw:3 * bw + LANES]
    lane = lax.broadcasted_iota(jnp.int32, lora.shape, 1)
    lora = jnp.where(lane < DECAY_LORA, jnp.tanh(lora), lora).astype(BF16)
    wa = _dot(lora, w2a2_ref[...])
    w_log = -jax.nn.softplus(-(w0_ref[...] + wa[:, :bw])) - 0.5
    w_ref[...] = jnp.exp(w_log)
    a_ref[...] = jax.nn.sigmoid(a0_ref[...] + wa[:, bw:])
    dg = jax.nn.sigmoid(ps[:, 3 * bw + LANES:]).astype(BF16)
    gr_ref[...] = _dot(dg, g2_ref[...])

    pb = _dot(h, wb_ref[...])
    sqt_ref[...] = (pb[:, 0:bw] * (HEAD_DIM ** -0.5 * LOG2E)).T.astype(BF16)
    sk_ref[...] = pb[:, bw:2 * bw].astype(BF16)
    _store_key_tiles(svt_ref, pb[:, 2 * bw:3 * bw].T.astype(BF16))
    off = 3 * bw
    cq = _rms(pb[:, off:off + MLA_Q_LORA], qg_ref[...]).astype(BF16)
    off += MLA_Q_LORA
    ckv = _rms(pb[:, off:off + MLA_KV_LORA], kvg_ref[...]).astype(BF16)
    off += MLA_KV_LORA
    cos = cos_ref[...]
    sin = sin_ref[...]
    kr = pb[:, off:off + LANES] * cos + pb[:, off + LANES:off + 2 * LANES] * sin
    qall = _dot(cq, wq_ref[...])
    kall = _dot(ckv, wk_ref[...])
    _store_key_tiles(mvt_ref, _dot(ckv, wv_ref[...]).T.astype(BF16))
    scale = (MLA_NOPE_DIM + MLA_ROPE_DIM) ** -0.5 * LOG2E
    for hd in range(N_HEADS):
        q = qall[:, 2 * hd * LANES:(2 * hd + 1) * LANES] * cos + qall[:, (2 * hd + 1) * LANES:(2 * hd + 2) * LANES] * sin
        mqt_ref[hd * LANES:(hd + 1) * LANES, :] = (q * scale).T.astype(BF16)
        mk_ref[hd] = (kall[:, hd * LANES:(hd + 1) * LANES] + kr).astype(BF16)


def _rotate_half_cols(w):
    half = MLA_ROPE_DIM // 2
    return jnp.concatenate([-w[..., half:], w[..., :half]], axis=-1)


def _prep_weights(mix_w_in, rwkv_mu, rwkv_w0, rwkv_a0, rwkv_w2, rwkv_a2, rwkv_g2,
                  mla_q_norm_g, mla_w_uq, mla_kv_norm_g, mla_w_ukv):
    depth, d, _ = mix_w_in.shape
    bw = BRANCH_W
    wa = jnp.pad(mix_w_in[..., :RWKV_COLS], ((0, 0), (0, 0), (0, RWKV_PAD - RWKV_COLS))).astype(BF16)
    mu = jnp.pad(rwkv_mu, ((0, 0), (0, RWKV_PAD - RWKV_COLS))).reshape(depth, 1, RWKV_PAD)
    o = RWKV_COLS
    qkv_cq_ckv = mix_w_in[..., o:o + 3 * bw + MLA_Q_LORA + MLA_KV_LORA]
    o += 3 * bw + MLA_Q_LORA + MLA_KV_LORA
    w_kr = mix_w_in[..., o:o + MLA_ROPE_DIM]
    o += MLA_ROPE_DIM
    z_lo = jnp.zeros((depth, d, MLA_NOPE_DIM), F32)
    z_hi = jnp.zeros((depth, d, LANES - MLA_NOPE_DIM - MLA_ROPE_DIM), F32)
    wb = jnp.concatenate([qkv_cq_ckv, z_lo, w_kr, z_hi, z_lo, _rotate_half_cols(w_kr), z_hi], axis=-1).astype(BF16)
    w_gate = mix_w_in[..., o:].astype(BF16)
    w2a2 = jnp.zeros((depth, LANES, 2 * bw), F32)
    w2a2 = w2a2.at[:, :DECAY_LORA, :bw].set(rwkv_w2).at[:, DECAY_LORA:, bw:].set(rwkv_a2).astype(BF16)
    g2 = jnp.pad(rwkv_g2, ((0, 0), (0, GATE_LORA_PAD - GATE_LORA), (0, 0))).astype(BF16)
    qd = MLA_NOPE_DIM + MLA_ROPE_DIM
    uq = mla_w_uq.reshape(depth, MLA_Q_LORA, N_HEADS, qd)
    zq = jnp.zeros((depth, MLA_Q_LORA, N_HEADS, LANES - qd), F32)
    zn = jnp.zeros((depth, MLA_Q_LORA, N_HEADS, MLA_NOPE_DIM), F32)
    wq = jnp.concatenate([uq, zq, zn, _rotate_half_cols(uq[..., MLA_NOPE_DIM:]), zq], axis=-1)
    wq = wq.reshape(depth, MLA_Q_LORA, N_HEADS * 2 * LANES).astype(BF16)
    ukv = mla_w_ukv.reshape(depth, MLA_KV_LORA, N_HEADS, MLA_NOPE_DIM + MLA_V_DIM)
    zk = jnp.zeros((depth, MLA_KV_LORA, N_HEADS, LANES - MLA_NOPE_DIM), F32)
    wk = jnp.concatenate([ukv[..., :MLA_NOPE_DIM], zk], axis=-1).reshape(depth, MLA_KV_LORA, N_HEADS * LANES).astype(BF16)
    wv = ukv[..., MLA_NOPE_DIM:].reshape(depth, MLA_KV_LORA, N_HEADS * MLA_V_DIM).astype(BF16)
    vec = lambda p: p.reshape(depth, 1, -1)
    prep = (wa, wb, mu, vec(rwkv_w0), vec(rwkv_a0), w2a2, g2, vec(mla_q_norm_g), vec(mla_kv_norm_g), wq, wk, wv)
    return prep, w_gate


def _prep_call(x, mod5, l, norm_g4, pw, cos_t, sin_t):
    bsz, seq, d = x.shape
    bw = BRANCH_W
    tm = _tile(seq, PREP_ROWS)
    tk = _tile(seq, ATT_TILE)
    assert tm % tk == 0
    row = lambda b, i: (b, i, 0)
    col = lambda b, i: (b, 0, i)
    xspec = pl.BlockSpec((None, tm, d), row)
    bspec = pl.BlockSpec((None, tm, bw), row)
    lspec = pl.BlockSpec((None, tm, LANES), row)
    f32_out = jax.ShapeDtypeStruct((bsz, seq, bw), F32)
    out_shape = (f32_out,) * 6 + (
        jax.ShapeDtypeStruct((bsz, bw, seq), BF16),
        jax.ShapeDtypeStruct((bsz, seq, bw), BF16),
        jax.ShapeDtypeStruct((bsz, seq // tk, bw, tk), BF16),
        jax.ShapeDtypeStruct((bsz, N_HEADS * LANES, seq), BF16),
        jax.ShapeDtypeStruct((bsz, N_HEADS, seq, LANES), BF16),
        jax.ShapeDtypeStruct((bsz, seq // tk, bw, tk), BF16),
    )
    vt_spec = pl.BlockSpec((None, tm // tk, bw, tk), lambda b, i: (b, i, 0, 0))
    out_specs = (bspec,) * 6 + (
        pl.BlockSpec((None, bw, tm), col),
        bspec,
        vt_spec,
        pl.BlockSpec((None, N_HEADS * LANES, tm), col),
        pl.BlockSpec((None, N_HEADS, tm, LANES), lambda b, i: (b, 0, i, 0)),
        vt_spec,
    )
    return pl.pallas_call(
        _prep_kernel,
        out_shape=out_shape,
        grid=(bsz, seq // tm),
        in_specs=[xspec, _mod_spec(l, 3, d), _mod_spec(l, 4, d), _layer_spec(norm_g4.shape, l, 1)]
                 + [_layer_spec(w.shape, l) for w in pw] + [lspec, lspec],
        out_specs=out_specs,
        scratch_shapes=[pltpu.VMEM((1, RWKV_PAD), F32)],
        compiler_params=_params(("parallel", "arbitrary")),
        name="mixer_prep",
    )(x, mod5, mod5, norm_g4, *pw, cos_t, sin_t)


def _sb_kernel(qt_ref, k_ref, vt_ref, tri_ref, o_ref, acc_ref, c_ref):
    qi = pl.program_id(1)
    tk = vt_ref.shape[2]
    tq = qt_ref.shape[1]
    heads = range(N_HEADS)
    frow = lax.broadcasted_iota(jnp.int32, (LANES, tq), 0)
    q_heads = []
    for hd in heads:
        qt = qt_ref[(hd // 2) * LANES:(hd // 2 + 1) * LANES, :]
        keep = frow < HEAD_DIM if hd % 2 == 0 else frow >= HEAD_DIM
        q_heads.append(jnp.where(keep, qt, jnp.zeros_like(qt)))
    acc_ref[...] = jnp.zeros_like(acc_ref)
    c_ref[...] = jnp.zeros_like(c_ref)

    def tiles(js, diag):
        tri = tri_ref[...]
        if diag:
            key = lax.broadcasted_iota(jnp.int32, (tk, tq), 0)
            qry = lax.broadcasted_iota(jnp.int32, (tk, tq), 1)
            mask = key < qry
        work = [(ji, hd) for ji in range(len(js)) for hd in heads]
        ks = [k_ref[pl.ds(pl.multiple_of(j * tk, tk), tk), :] for j in js]
        zs = [_dot(ks[ji][:, (hd // 2) * LANES:(hd // 2 + 1) * LANES], q_heads[hd]) for ji, hd in work]
        sps, lss = [], []
        for z in zs:
            neg_abs = lax.bitcast_convert_type(lax.bitcast_convert_type(z, jnp.uint32) | jnp.uint32(0x80000000), F32)
            ls = jnp.minimum(z, 0.0) - jnp.log2(1.0 + jnp.exp2(neg_abs))
            sp = z - ls
            if diag:
                sp = jnp.where(mask, sp, 0.0)
            sps.append(sp)
            lss.append(ls)
        afters = []
        for sp in sps:
            hi = sp.astype(BF16)
            lo = (sp - hi.astype(F32)).astype(BF16)
            afters.append(_dot(tri, jnp.concatenate([hi, lo], axis=0)))
        cs = [c_ref[hd] for hd in heads]
        ws = []
        for i, (ji, hd) in enumerate(work):
            w = jnp.exp2(lss[i] - afters[i][:tk] - cs[hd])
            if diag:
                w = jnp.where(mask, w, 0.0)
            ws.append(w.astype(BF16))
            cs[hd] = cs[hd] + afters[i][tk:tk + 1]
        for hd in heads:
            c_ref[hd] = cs[hd]
        vts = [vt_ref[j] for j in js]
        for hd in heads:
            rows = slice(hd * HEAD_DIM, (hd + 1) * HEAD_DIM)
            upd = None
            for i, (ji, h2) in enumerate(work):
                if h2 == hd:
                    part = _dot(vts[ji][rows, :], ws[i])
                    upd = part if upd is None else upd + part
            acc_ref[rows, :] += upd

    tiles([qi], True)

    def body(jj, carry):
        j = qi - 1 - 2 * jj
        tiles([j, j - 1], False)
        return carry

    lax.fori_loop(0, qi // 2, body, 0)

    @pl.when(qi % 2 == 1)
    def _():
        tiles([0], False)

    o_ref[...] = acc_ref[...].T.astype(o_ref.dtype)


def _sb_call(qt, k, vt):
    bsz, seq, bw = k.shape
    nk, tk = vt.shape[1], vt.shape[3]
    tq = tk
    tri = lax.broadcasted_iota(jnp.int32, (tk, tk), 1) > lax.broadcasted_iota(jnp.int32, (tk, tk), 0)
    tri = jnp.concatenate([tri, jnp.ones((SUBLANES, tk), jnp.bool_)], axis=0).astype(BF16)
    tri = jnp.concatenate([tri, tri], axis=1)
    return pl.pallas_call(
        _sb_kernel,
        out_shape=jax.ShapeDtypeStruct(k.shape, BF16),
        grid=(bsz, seq // tq),
        in_specs=[
            pl.BlockSpec((None, bw, tq), lambda b, i: (b, 0, i)),
            pl.BlockSpec((None, seq, bw), lambda b, i: (b, 0, 0)),
            pl.BlockSpec((None, nk, bw, tk), lambda b, i: (b, 0, 0, 0)),
            _const_spec(tri.shape),
        ],
        out_specs=pl.BlockSpec((None, tq, bw), lambda b, i: (b, i, 0)),
        scratch_shapes=[pltpu.VMEM((bw, tq), F32), pltpu.VMEM((N_HEADS, 1, tq), F32)],
        compiler_params=_params(("parallel", "parallel")),
        name="stick_breaking",
    )(qt, k, vt, tri)


def _mla_kernel(qt_ref, k_ref, vt_ref, o_ref, acc_ref, m_ref, l_ref):
    qi = pl.program_id(1)
    tk = vt_ref.shape[2]
    tq = qt_ref.shape[1]
    heads = range(N_HEADS)
    acc_ref[...] = jnp.zeros_like(acc_ref)
    m_ref[...] = jnp.full_like(m_ref, NEG_BIG)
    l_ref[...] = jnp.zeros_like(l_ref)

    def tiles(j, nt, diag):
        start = pl.multiple_of(j * tk, tk)
        vt = jnp.concatenate([vt_ref[j + i] for i in range(nt)], axis=1) if nt > 1 else vt_ref[j]
        if diag:
            key = lax.broadcasted_iota(jnp.int32, (tk, tq), 0)
            qry = lax.broadcasted_iota(jnp.int32, (tk, tq), 1)
            mask = key <= qry
        ss = [_dot(k_ref[hd, pl.ds(start, nt * tk), :], qt_ref[hd * LANES:(hd + 1) * LANES, :])
              for hd in heads]
        if diag:
            ss = [jnp.where(mask, s, NEG_BIG) for s in ss]
        ps, alphas = [], []
        for hd in heads:
            m_old = m_ref[hd]
            m_new = jnp.maximum(m_old, jnp.max(ss[hd], axis=0, keepdims=True))
            p = jnp.exp2(ss[hd] - m_new)
            alpha = jnp.exp2(m_old - m_new)
            l_ref[hd] = alpha * l_ref[hd] + jnp.sum(p, axis=0, keepdims=True)
            m_ref[hd] = m_new
            ps.append(p.astype(BF16))
            alphas.append(alpha)
        for hd in heads:
            rows = slice(hd * HEAD_DIM, (hd + 1) * HEAD_DIM)
            acc_ref[rows, :] = alphas[hd] * acc_ref[rows, :] + _dot(vt[rows, :], ps[hd])

    tiles(qi, 1, True)

    def body(jj, carry):
        tiles(2 * jj, 2, False)
        return carry

    lax.fori_loop(0, qi // 2, body, 0)

    @pl.when(qi % 2 == 1)
    def _():
        tiles(qi - 1, 1, False)

    out = jnp.concatenate(
        [acc_ref[hd * HEAD_DIM:(hd + 1) * HEAD_DIM, :] / l_ref[hd] for hd in heads], axis=0)
    o_ref[...] = out.T.astype(o_ref.dtype)


def _mla_call(qt, k, vt):
    bsz, _, seq, _ = k.shape
    nk, bw, tk = vt.shape[1:]
    tq = tk
    return pl.pallas_call(
        _mla_kernel,
        out_shape=jax.ShapeDtypeStruct((bsz, seq, bw), BF16),
        grid=(bsz, seq // tq),
        in_specs=[
            pl.BlockSpec((None, N_HEADS * LANES, tq), lambda b, i: (b, 0, i)),
            pl.BlockSpec((None, N_HEADS, seq, LANES), lambda b, i: (b, 0, 0, 0)),
            pl.BlockSpec((None, nk, bw, tk), lambda b, i: (b, 0, 0, 0)),
        ],
        out_specs=pl.BlockSpec((None, tq, bw), lambda b, i: (b, i, 0)),
        scratch_shapes=[pltpu.VMEM((bw, tq), F32), pltpu.VMEM((N_HEADS, 1, tq), F32),
                        pltpu.VMEM((N_HEADS, 1, tq), F32)],
        compiler_params=_params(("parallel", "parallel")),
        name="latent_attention",
    )(qt, k, vt)


def _scan_kernel(r_ref, e_ref, k_ref, v_ref, a_ref, kk_ref, ka_ref, rk_ref, lg_ref, lb_ref,
                 o_ref, h_ref, vec_ref, cum_ref):
    n = HEAD_DIM

    @pl.when(pl.program_id(0) == 0)
    def _():
        h_ref[...] = jnp.zeros_like(h_ref)

    cum_ref[...] = jnp.zeros_like(cum_ref)

    def bcast_row(ref, c):
        return ref[pl.ds(c, 1), :]

    def step(t, carry):
        kt = k_ref[t]
        at = a_ref[t]
        rt = r_ref[t]
        vt = v_ref[t]
        e_prev = cum_ref[...]
        e_cum = e_prev + e_ref[t]
        cum_ref[...] = e_cum
        g_inv = jnp.exp(e_cum)
        kk = kt * kk_ref[...]
        kk = kk * lax.rsqrt(jnp.maximum(jnp.sum(kk * kk, axis=0, keepdims=True), 1e-24))
        km = kt * (1.0 + (at - 1.0) * ka_ref[...])
        vec_ref[0] = kk * jnp.exp(-e_prev)
        vec_ref[1] = kk * at * g_inv
        vec_ref[2] = km * g_inv
        vec_ref[3] = rt * jnp.exp(-e_cum)
        bonus = jnp.sum(rt * km * rk_ref[...], axis=0, keepdims=True)
        sa_parts = [jnp.zeros_like(vt) for _ in range(SCAN_PARTIALS)]
        for c in range(n):
            sa_parts[c % SCAN_PARTIALS] = sa_parts[c % SCAN_PARTIALS] - h_ref[c] * bcast_row(vec_ref.at[0], c)
        sa = functools.reduce(lambda p, q: p + q, sa_parts)
        y_parts = [jnp.zeros_like(vt) for _ in range(SCAN_PARTIALS)]
        for c in range(n):
            hn = h_ref[c] + sa * bcast_row(vec_ref.at[1], c) + vt * bcast_row(vec_ref.at[2], c)
            h_ref[c] = hn
            y_parts[c % SCAN_PARTIALS] = y_parts[c % SCAN_PARTIALS] + hn * bcast_row(vec_ref.at[3], c)
        y = functools.reduce(lambda p, q: p + q, y_parts)
        mean = jnp.mean(y, axis=(0, 1), keepdims=True)
        yc = y - mean
        var = jnp.mean(yc * yc, axis=(0, 1), keepdims=True)
        yn = yc * lax.rsqrt(var + RWKV_GN_EPS) * lg_ref[...] + lb_ref[...]
        o_ref[t] = yn + bonus * vt
        return carry

    lax.fori_loop(0, r_ref.shape[0], step, 0)
    vec_ref[0] = jnp.exp(-cum_ref[...])
    for c in range(n):
        h_ref[c] = h_ref[c] * bcast_row(vec_ref.at[0], c)


def _to_chains(t):
    bsz, seq, _ = t.shape
    return t.reshape(bsz, seq, N_HEADS, HEAD_DIM).transpose(1, 3, 0, 2).reshape(seq, HEAD_DIM, bsz * N_HEADS)


def _param_chains(p, bsz):
    return jnp.tile(p.reshape(N_HEADS, HEAD_DIM).T, (1, bsz))


def _scan_call(r, w, k, v, a, k_k, k_a, r_k, lnx_g, lnx_b):
    bsz, seq, _ = r.shape
    chains = bsz * N_HEADS
    n = HEAD_DIM
    sub = n // SUBLANES
    r, w, k, v, a = (_to_chains(t) for t in (r, w, k, v, a))
    v = v.reshape(seq, sub, SUBLANES, chains)
    k_k, k_a, r_k, lnx_g, lnx_b = (_param_chains(p.reshape(-1), bsz) for p in (k_k, k_a, r_k, lnx_g, lnx_b))
    lnx_g = lnx_g.reshape(sub, SUBLANES, chains)
    lnx_b = lnx_b.reshape(sub, SUBLANES, chains)
    ts = _tile(seq, SCAN_STEPS)
    tspec = pl.BlockSpec((ts, n, chains), lambda i: (i, 0, 0))
    vspec = pl.BlockSpec((ts, sub, SUBLANES, chains), lambda i: (i, 0, 0, 0))
    pspec = _const_spec((n, chains))
    gspec = _const_spec((sub, SUBLANES, chains))
    out = pl.pallas_call(
        _scan_kernel,
        out_shape=jax.ShapeDtypeStruct((seq, sub, SUBLANES, chains), F32),
        grid=(seq // ts,),
        in_specs=[tspec, tspec, tspec, vspec, tspec, pspec, pspec, pspec, gspec, gspec],
        out_specs=vspec,
        scratch_shapes=[pltpu.VMEM((n, sub, SUBLANES, chains), F32), pltpu.VMEM((4, n, chains), F32),
                        pltpu.VMEM((n, chains), F32)],
        compiler_params=_params(("arbitrary",)),
        name="rwkv7_scan",
    )(r, w, k, v, a, k_k, k_a, r_k, lnx_g, lnx_b)
    out = out.reshape(seq, n, bsz, N_HEADS).transpose(2, 0, 3, 1)
    return out.reshape(bsz, seq, BRANCH_W)


def _merge_body(x, shift, scale, gate_mod, g, wg_ref, ya, gr, yb, yc, bw_ref, wo_ref):
    d = x.shape[1]
    h = _modulated_norm(x, g, shift, scale).astype(BF16)
    ys = ((ya * gr).astype(BF16), yb, yc)
    merged = None
    for n in range(N_BRANCHES):
        gate = jax.nn.sigmoid(_dot(h, wg_ref[:, n * d:(n + 1) * d]))
        term = gate * _dot(ys[n], bw_ref[n])
        merged = term if merged is None else merged + term
    return x + gate_mod * _dot(merged.astype(BF16), wo_ref[...])


def _merge_ffn_kernel(x_ref, sh1_ref, sc1_ref, gt1_ref, sh2_ref, sc2_ref, gt2_ref, g1_ref, g2_ref,
                      wg_ref, ya_ref, gr_ref, yb_ref, yc_ref, bw_ref, wo_ref, wi2_ref, wo2_ref, *rest, final):
    if final:
        fg_ref, o_ref = rest
    else:
        (o_ref,) = rest
    x = _merge_body(x_ref[...], sh1_ref[...], sc1_ref[...], gt1_ref[...], g1_ref[...], wg_ref,
                    ya_ref[...], gr_ref[...], yb_ref[...], yc_ref[...], bw_ref, wo_ref)
    out = _ffn_body(x, sh2_ref[...], sc2_ref[...], gt2_ref[...], g2_ref[...], wi2_ref, wo2_ref)
    if final:
        out = _rms(out, fg_ref[...])
    o_ref[...] = out


def _merge_ffn_call(x, mod5, l, norm_g4, w_gate, ya, gr, yb, yc, branch_w, w_out, w_in2, w_out2, final_g=None):
    bsz, seq, d = x.shape
    bw = BRANCH_W
    tm = _tile(seq, FFN_ROWS)
    final = final_g is not None
    row = lambda b, i: (b, i, 0)
    xspec = pl.BlockSpec((None, tm, d), row)
    bspec = pl.BlockSpec((None, tm, bw), row)
    in_specs = [xspec] + [_mod_spec(l, j, d) for j in range(3, 9)] + [
        _layer_spec(norm_g4.shape, l, 1), _layer_spec(norm_g4.shape, l, 2),
        _layer_spec(w_gate.shape, l), bspec, bspec, bspec, bspec,
        _layer_spec(branch_w.shape, l), _layer_spec(w_out.shape, l),
        _layer_spec(w_in2.shape, l), _layer_spec(w_out2.shape, l),
    ]
    args = [x] + [mod5] * 6 + [norm_g4, norm_g4, w_gate, ya, gr, yb, yc, branch_w, w_out, w_in2, w_out2]
    if final:
        in_specs.append(_const_spec((1, d)))
        args.append(final_g.reshape(1, d))
    return pl.pallas_call(
        functools.partial(_merge_ffn_kernel, final=final),
        out_shape=jax.ShapeDtypeStruct(x.shape, F32),
        grid=(bsz, seq // tm),
        in_specs=in_specs,
        out_specs=xspec,
        compiler_params=_params(("parallel", "parallel")),
        name="merge_ffn",
    )(*args)


def kernel(x, c, positions, ada_w, ada_b, norm_g, ffn1_w_in, ffn1_w_out, mix_w_in, rwkv_mu, rwkv_w0, rwkv_w2, rwkv_a0, rwkv_a2, rwkv_g2, rwkv_k_k, rwkv_k_a, rwkv_r_k, rwkv_lnx_g, rwkv_lnx_b, mla_q_norm_g, mla_w_uq, mla_kv_norm_g, mla_w_ukv, branch_w, mix_w_out, ffn2_w_in, ffn2_w_out, final_norm_g):
    depth, _, d = norm_g.shape
    mod5 = _ada_call(c, ada_w, ada_b)
    cos_t, sin_t = _rope_call(positions)
    norm_g4 = norm_g.reshape(depth, 3, 1, d)
    w1_in, w1_out, w2_in, w2_out, w_mix_out = (
        _cast_call(w) for w in (ffn1_w_in, ffn1_w_out, ffn2_w_in, ffn2_w_out, mix_w_out))
    w_branch = _cast_call(branch_w.reshape(depth * N_BRANCHES, BRANCH_W, d), rows=BRANCH_W).reshape(branch_w.shape)
    pw, w_gate = _prep_weights(mix_w_in, rwkv_mu, rwkv_w0, rwkv_a0, rwkv_w2, rwkv_a2, rwkv_g2,
                               mla_q_norm_g, mla_w_uq, mla_kv_norm_g, mla_w_ukv)
    for l in range(depth):
        x = _ffn_call(x, mod5, l, norm_g4, w1_in, w1_out)
        (r, k, v, w, a, gr, sqt, sk, svt, mqt, mk, mvt) = _prep_call(x, mod5, l, norm_g4, pw, cos_t, sin_t)
        ya = _scan_call(r, w, k, v, a, rwkv_k_k[l], rwkv_k_a[l], rwkv_r_k[l], rwkv_lnx_g[l], rwkv_lnx_b[l])
        yb = _sb_call(sqt, sk, svt)
        yc = _mla_call(mqt, mk, mvt)
        x = _merge_ffn_call(x, mod5, l, norm_g4, w_gate, ya, gr, yb, yc, w_branch, w_mix_out, w2_in, w2_out,
                            final_g=final_norm_g if l == depth - 1 else None)
    return x
```

```python
import functools
import math

import jax
import jax.numpy as jnp
from jax import lax
from jax.experimental import pallas as pl
from jax.experimental.pallas import tpu as pltpu

F32 = jnp.float32
BF16 = jnp.bfloat16

HEAD_DIM = 64
N_HEADS = 6
BRANCH_W = N_HEADS * HEAD_DIM
N_BRANCHES = 3
DECAY_LORA = 64
AAA_LORA = 64
GATE_LORA = 160
RWKV_GN_EPS = 64e-5
MLA_Q_LORA = 256
MLA_KV_LORA = 128
MLA_NOPE_DIM = 64
MLA_ROPE_DIM = 32
MLA_V_DIM = 64
ROPE_THETA = 10000.0
MACARON_WEIGHT = 0.5
NORM_EPS = 1e-6
N_ADA = 9
RWKV_COLS = 3 * BRANCH_W + DECAY_LORA + AAA_LORA + GATE_LORA

LANES = 128
SUBLANES = 8
MXU_WIDTH = 256
VMEM_LIMIT_BYTES = 56 * 1024 * 1024

FFN_ROWS = 512
PREP_ROWS = 512
ATT_TILE = 256
CAST_ROWS = 256
SCAN_STEPS = 64
SCAN_UNROLL = 2
SCAN_PARTIALS = 2

RWKV_PAD = 1536
GATE_LORA_PAD = RWKV_PAD - (3 * BRANCH_W + DECAY_LORA + AAA_LORA)
NEG_BIG = -1e30
LOG2E = math.log2(math.e)


def _tile(n, pref):
    t = min(n, pref)
    assert n % t == 0, (n, t)
    return t


def _params(sem):
    return pltpu.CompilerParams(dimension_semantics=sem, vmem_limit_bytes=VMEM_LIMIT_BYTES)


def _const_spec(shape):
    nd = len(shape)
    return pl.BlockSpec(shape, lambda *_: (0,) * nd)


def _resident_spec(shape):
    nd = len(shape)
    return pl.BlockSpec(shape, lambda *_: (0,) * nd, pipeline_mode=pl.Buffered(1))


def _layer_spec(shape, *lead):
    n_lead = len(lead)
    tail = tuple(shape[n_lead:])
    return pl.BlockSpec((None,) * n_lead + tail, lambda *_: tuple(lead) + (0,) * len(tail),
                        pipeline_mode=pl.Buffered(1))


def _mod_spec(l, j, d):
    return pl.BlockSpec((None, None, None, 1, d), lambda b, *_: (l, j, b, 0, 0))


def _modulated_norm(x, g, shift, scale):
    ms = jnp.mean(x * x, axis=-1, keepdims=True)
    y = x * lax.rsqrt(ms + NORM_EPS) * g
    return y * (1.0 + scale) + shift


def _rms(x, g):
    ms = jnp.mean(x * x, axis=-1, keepdims=True)
    return x * lax.rsqrt(ms + NORM_EPS) * g


def _dot(a, b):
    return jnp.dot(a, b, preferred_element_type=F32)


def _ada_kernel(c_ref, w_ref, b_ref, o_ref):
    c = c_ref[...]
    d = c.shape[1]
    ca = (c * jax.nn.sigmoid(c)).astype(BF16)
    res = _dot(ca, w_ref[...].astype(BF16))
    for j in range(o_ref.shape[0]):
        o_ref[j] = res[:, j * d:(j + 1) * d] + b_ref[j]


def _ada_call(c, ada_w, ada_b):
    depth, d, _ = ada_w.shape
    bsz = c.shape[0]
    per = 3
    ada_b4 = ada_b.reshape(depth, N_ADA, 1, d)
    out = pl.pallas_call(
        _ada_kernel,
        out_shape=jax.ShapeDtypeStruct((depth, N_ADA, bsz, d), F32),
        grid=(depth, N_ADA // per),
        in_specs=[
            pl.BlockSpec((bsz, d), lambda l, j: (0, 0)),
            pl.BlockSpec((None, d, per * d), lambda l, j: (l, 0, j)),
            pl.BlockSpec((None, per, 1, d), lambda l, j: (l, j, 0, 0)),
        ],
        out_specs=pl.BlockSpec((None, per, bsz, d), lambda l, j: (l, j, 0, 0)),
        compiler_params=_params(("parallel", "parallel")),
        name="ada_mod",
    )(c, ada_w, ada_b4)
    return out.reshape(depth, N_ADA, bsz, 1, d)


def _rope_kernel(pos_ref, freq_ref, cos_ref, sin_ref):
    ang = pos_ref[...].astype(F32) * freq_ref[...]
    lane = lax.broadcasted_iota(jnp.int32, ang.shape, 1)
    in_rope = (lane >= MLA_NOPE_DIM) & (lane < MLA_NOPE_DIM + MLA_ROPE_DIM)
    cos_ref[...] = jnp.where(lane < MLA_NOPE_DIM, 1.0, jnp.where(in_rope, jnp.cos(ang), 0.0))
    sin_ref[...] = jnp.where(in_rope, jnp.sin(ang), 0.0)


def _rope_call(positions):
    bsz, seq = positions.shape
    half = MLA_ROPE_DIM // 2
    inv_freq = 1.0 / (ROPE_THETA ** (jnp.arange(0, MLA_ROPE_DIM, 2, dtype=F32) / MLA_ROPE_DIM))
    freq = jnp.zeros((1, LANES), F32)
    freq = freq.at[0, MLA_NOPE_DIM:MLA_NOPE_DIM + half].set(inv_freq)
    freq = freq.at[0, MLA_NOPE_DIM + half:MLA_NOPE_DIM + 2 * half].set(inv_freq)
    ts = _tile(seq, 512)
    spec = pl.BlockSpec((None, ts, LANES), lambda b, i: (b, i, 0))
    return pl.pallas_call(
        _rope_kernel,
        out_shape=(jax.ShapeDtypeStruct((bsz, seq, LANES), F32),) * 2,
        grid=(bsz, seq // ts),
        in_specs=[pl.BlockSpec((None, ts, 1), lambda b, i: (b, i, 0)), _const_spec((1, LANES))],
        out_specs=(spec, spec),
        compiler_params=_params(("parallel", "parallel")),
        name="rope_tables",
    )(positions.reshape(bsz, seq, 1), freq)


def _ffn_body(x, shift, scale, gate_mod, g, wi_ref, wo_ref):
    d_ff = wo_ref.shape[0]
    assert d_ff % MXU_WIDTH == 0
    h = _modulated_norm(x, g, shift, scale).astype(BF16)
    acc = None
    for c0 in range(0, d_ff, MXU_WIDTH):
        gate = _dot(h, wi_ref[:, c0:c0 + MXU_WIDTH])
        up = _dot(h, wi_ref[:, d_ff + c0:d_ff + c0 + MXU_WIDTH])
        act = (gate * jax.nn.sigmoid(gate) * up).astype(BF16)
        part = _dot(act, wo_ref[c0:c0 + MXU_WIDTH, :])
        acc = part if acc is None else acc + part
    return x + (MACARON_WEIGHT * gate_mod) * acc


def _ffn_kernel(x_ref, sh_ref, sc_ref, gt_ref, g_ref, wi_ref, wo_ref, o_ref):
    o_ref[...] = _ffn_body(x_ref[...], sh_ref[...], sc_ref[...], gt_ref[...], g_ref[...], wi_ref, wo_ref)


def _ffn_call(x, mod5, l, norm_g4, w_in, w_out):
    bsz, seq, d = x.shape
    tm = _tile(seq, FFN_ROWS)
    xspec = pl.BlockSpec((None, tm, d), lambda b, i: (b, i, 0))
    return pl.pallas_call(
        _ffn_kernel,
        out_shape=jax.ShapeDtypeStruct((bsz, seq, d), F32),
        grid=(bsz, seq // tm),
        in_specs=[
            xspec, _mod_spec(l, 0, d), _mod_spec(l, 1, d), _mod_spec(l, 2, d),
            _layer_spec(norm_g4.shape, l, 0), _layer_spec(w_in.shape, l), _layer_spec(w_out.shape, l),
        ],
        out_specs=xspec,
        compiler_params=_params(("parallel", "parallel")),
        name="ffn",
    )(x, mod5, mod5, mod5, norm_g4, w_in, w_out)


def _cast_kernel(x_ref, o_ref):
    o_ref[...] = x_ref[...].astype(o_ref.dtype)


def _cast_call(w, rows=CAST_ROWS):
    depth, r, c = w.shape
    tr = _tile(r, rows)
    spec = pl.BlockSpec((None, tr, c), lambda l, i: (l, i, 0))
    return pl.pallas_call(
        _cast_kernel,
        out_shape=jax.ShapeDtypeStruct(w.shape, BF16),
        grid=(depth, r // tr),
        in_specs=[spec],
        out_specs=spec,
        compiler_params=_params(("parallel", "parallel")),
        name="cast_bf16",
    )(w)


def _store_key_tiles(vt_ref, vt):
    tk = vt_ref.shape[2]
    for j in range(vt_ref.shape[0]):
        vt_ref[j] = vt[:, j * tk:(j + 1) * tk]


def _prep_kernel(x_ref, sh_ref, sc_ref, g_ref, wa_ref, wb_ref, mu_ref, w0_ref, a0_ref,
                 w2a2_ref, g2_ref, qg_ref, kvg_ref, wq_ref, wk_ref, wv_ref, cos_ref, sin_ref,
                 r_ref, k_ref, v_ref, w_ref, a_ref, gr_ref,
                 sqt_ref, sk_ref, svt_ref, mqt_ref, mk_ref, mvt_ref,
                 carry_ref):
    i = pl.program_id(1)
    bw = BRANCH_W

    @pl.when(i == 0)
    def _():
        carry_ref[...] = jnp.zeros_like(carry_ref)

    h = _modulated_norm(x_ref[...], g_ref[...], sh_ref[...], sc_ref[...]).astype(BF16)
    tm = h.shape[0]

    p = _dot(h, wa_ref[...])
    row = lax.broadcasted_iota(jnp.int32, p.shape, 0)
    prev = jnp.where(row == 0, carry_ref[...], pltpu.roll(p, 1, 0))
    carry_ref[...] = p[tm - 1:tm, :]
    ps = p + (prev - p) * mu_ref[...]
    r_ref[...] = ps[:, 0:bw]
    k_ref[...] = ps[:, bw:2 * bw]
    v_ref[...] = ps[:, 2 * bw:3 * bw]
    lora = ps[:, 3 * bw:3 * bw + LANES]
    lane = lax.broadcasted_iota(jnp.int32, lora.shape, 1)
    lora = jnp.where(lane < DECAY_LORA, jnp.tanh(lora), lora).astype(BF16)
    wa = _dot(lora, w2a2_ref[...])
    w_log = -jax.nn.softplus(-(w0_ref[...] + wa[:, :bw])) - 0.5
    w_ref[...] = jnp.exp(w_log)
    a_ref[...] = jax.nn.sigmoid(a0_ref[...] + wa[:, bw:])
    dg = jax.nn.sigmoid(ps[:, 3 * bw + LANES:]).astype(BF16)
    gr_ref[...] = _dot(dg, g2_ref[...])

    pb = _dot(h, wb_ref[...])
    sqt_ref[...] = (pb[:, 0:bw] * (HEAD_DIM ** -0.5 * LOG2E)).T.astype(BF16)
    sk_ref[...] = pb[:, bw:2 * bw].astype(BF16)
    _store_key_tiles(svt_ref, pb[:, 2 * bw:3 * bw].T.astype(BF16))
    off = 3 * bw
    cq = _rms(pb[:, off:off + MLA_Q_LORA], qg_ref[...]).astype(BF16)
    off += MLA_Q_LORA
    ckv = _rms(pb[:, off:off + MLA_KV_LORA], kvg_ref[...]).astype(BF16)
    off += MLA_KV_LORA
    cos = cos_ref[...]
    sin = sin_ref[...]
    kr = pb[:, off:off + LANES] * cos + pb[:, off + LANES:off + 2 * LANES] * sin
    qall = _dot(cq, wq_ref[...])
    kall = _dot(ckv, wk_ref[...])
    _store_key_tiles(mvt_ref, _dot(ckv, wv_ref[...]).T.astype(BF16))
    scale = (MLA_NOPE_DIM + MLA_ROPE_DIM) ** -0.5 * LOG2E
    for hd in range(N_HEADS):
        q = qall[:, 2 * hd * LANES:(2 * hd + 1) * LANES] * cos + qall[:, (2 * hd + 1) * LANES:(2 * hd + 2) * LANES] * sin
        mqt_ref[hd * LANES:(hd + 1) * LANES, :] = (q * scale).T.astype(BF16)
        mk_ref[hd] = (kall[:, hd * LANES:(hd + 1) * LANES] + kr).astype(BF16)


def _rotate_half_cols(w):
    half = MLA_ROPE_DIM // 2
    return jnp.concatenate([-w[..., half:], w[..., :half]], axis=-1)


def _prep_weights(mix_w_in, rwkv_mu, rwkv_w0, rwkv_a0, rwkv_w2, rwkv_a2, rwkv_g2,
                  mla_q_norm_g, mla_w_uq, mla_kv_norm_g, mla_w_ukv):
    depth, d, _ = mix_w_in.shape
    bw = BRANCH_W
    wa = jnp.pad(mix_w_in[..., :RWKV_COLS], ((0, 0), (0, 0), (0, RWKV_PAD - RWKV_COLS))).astype(BF16)
    mu = jnp.pad(rwkv_mu, ((0, 0), (0, RWKV_PAD - RWKV_COLS))).reshape(depth, 1, RWKV_PAD)
    o = RWKV_COLS
    qkv_cq_ckv = mix_w_in[..., o:o + 3 * bw + MLA_Q_LORA + MLA_KV_LORA]
    o += 3 * bw + MLA_Q_LORA + MLA_KV_LORA
    w_kr = mix_w_in[..., o:o + MLA_ROPE_DIM]
    o += MLA_ROPE_DIM
    z_lo = jnp.zeros((depth, d, MLA_NOPE_DIM), F32)
    z_hi = jnp.zeros((depth, d, LANES - MLA_NOPE_DIM - MLA_ROPE_DIM), F32)
    wb = jnp.concatenate([qkv_cq_ckv, z_lo, w_kr, z_hi, z_lo, _rotate_half_cols(w_kr), z_hi], axis=-1).astype(BF16)
    w_gate = mix_w_in[..., o:].astype(BF16)
    w2a2 = jnp.zeros((depth, LANES, 2 * bw), F32)
    w2a2 = w2a2.at[:, :DECAY_LORA, :bw].set(rwkv_w2).at[:, DECAY_LORA:, bw:].set(rwkv_a2).astype(BF16)
    g2 = jnp.pad(rwkv_g2, ((0, 0), (0, GATE_LORA_PAD - GATE_LORA), (0, 0))).astype(BF16)
    qd = MLA_NOPE_DIM + MLA_ROPE_DIM
    uq = mla_w_uq.reshape(depth, MLA_Q_LORA, N_HEADS, qd)
    zq = jnp.zeros((depth, MLA_Q_LORA, N_HEADS, LANES - qd), F32)
    zn = jnp.zeros((depth, MLA_Q_LORA, N_HEADS, MLA_NOPE_DIM), F32)
    wq = jnp.concatenate([uq, zq, zn, _rotate_half_cols(uq[..., MLA_NOPE_DIM:]), zq], axis=-1)
    wq = wq.reshape(depth, MLA_Q_LORA, N_HEADS * 2 * LANES).astype(BF16)
    ukv = mla_w_ukv.reshape(depth, MLA_KV_LORA, N_HEADS, MLA_NOPE_DIM + MLA_V_DIM)
    zk = jnp.zeros((depth, MLA_KV_LORA, N_HEADS, LANES - MLA_NOPE_DIM), F32)
    wk = jnp.concatenate([ukv[..., :MLA_NOPE_DIM], zk], axis=-1).reshape(depth, MLA_KV_LORA, N_HEADS * LANES).astype(BF16)
    wv = ukv[..., MLA_NOPE_DIM:].reshape(depth, MLA_KV_LORA, N_HEADS * MLA_V_DIM).astype(BF16)
    vec = lambda p: p.reshape(depth, 1, -1)
    prep = (wa, wb, mu, vec(rwkv_w0), vec(rwkv_a0), w2a2, g2, vec(mla_q_norm_g), vec(mla_kv_norm_g), wq, wk, wv)
    return prep, w_gate


def _prep_call(x, mod5, l, norm_g4, pw, cos_t, sin_t):
    bsz, seq, d = x.shape
    bw = BRANCH_W
    tm = _tile(seq, PREP_ROWS)
    tk = _tile(seq, ATT_TILE)
    assert tm % tk == 0
    row = lambda b, i: (b, i, 0)
    col = lambda b, i: (b, 0, i)
    xspec = pl.BlockSpec((None, tm, d), row)
    bspec = pl.BlockSpec((None, tm, bw), row)
    lspec = pl.BlockSpec((None, tm, LANES), row)
    f32_out = jax.ShapeDtypeStruct((bsz, seq, bw), F32)
    out_shape = (f32_out,) * 6 + (
        jax.ShapeDtypeStruct((bsz, bw, seq), BF16),
        jax.ShapeDtypeStruct((bsz, seq, bw), BF16),
        jax.ShapeDtypeStruct((bsz, seq // tk, bw, tk), BF16),
        jax.ShapeDtypeStruct((bsz, N_HEADS * LANES, seq), BF16),
        jax.ShapeDtypeStruct((bsz, N_HEADS, seq, LANES), BF16),
        jax.ShapeDtypeStruct((bsz, seq // tk, bw, tk), BF16),
    )
    vt_spec = pl.BlockSpec((None, tm // tk, bw, tk), lambda b, i: (b, i, 0, 0))
    out_specs = (bspec,) * 6 + (
        pl.BlockSpec((None, bw, tm), col),
        bspec,
        vt_spec,
        pl.BlockSpec((None, N_HEADS * LANES, tm), col),
        pl.BlockSpec((None, N_HEADS, tm, LANES), lambda b, i: (b, 0, i, 0)),
        vt_spec,
    )
    return pl.pallas_call(
        _prep_kernel,
        out_shape=out_shape,
        grid=(bsz, seq // tm),
        in_specs=[xspec, _mod_spec(l, 3, d), _mod_spec(l, 4, d), _layer_spec(norm_g4.shape, l, 1)]
                 + [_layer_spec(w.shape, l) for w in pw] + [lspec, lspec],
        out_specs=out_specs,
        scratch_shapes=[pltpu.VMEM((1, RWKV_PAD), F32)],
        compiler_params=_params(("parallel", "arbitrary")),
        name="mixer_prep",
    )(x, mod5, mod5, norm_g4, *pw, cos_t, sin_t)


def _sb_kernel(qt_ref, k_ref, vt_ref, tri_ref, o_ref, acc_ref, c_ref):
    qi = pl.program_id(1)
    tk = vt_ref.shape[2]
    tq = qt_ref.shape[1]
    heads = range(N_HEADS)
    frow = lax.broadcasted_iota(jnp.int32, (LANES, tq), 0)
    q_heads = []
    for hd in heads:
        qt = qt_ref[(hd // 2) * LANES:(hd // 2 + 1) * LANES, :]
        keep = frow < HEAD_DIM if hd % 2 == 0 else frow >= HEAD_DIM
        q_heads.append(jnp.where(keep, qt, jnp.zeros_like(qt)))
    acc_ref[...] = jnp.zeros_like(acc_ref)
    c_ref[...] = jnp.zeros_like(c_ref)

    def tiles(js, diag):
        tri = tri_ref[...]
        if diag:
            key = lax.broadcasted_iota(jnp.int32, (tk, tq), 0)
            qry = lax.broadcasted_iota(jnp.int32, (tk, tq), 1)
            mask = key < qry
        work = [(ji, hd) for ji in range(len(js)) for hd in heads]
        ks = [k_ref[pl.ds(pl.multiple_of(j * tk, tk), tk), :] for j in js]
        zs = [_dot(ks[ji][:, (hd // 2) * LANES:(hd // 2 + 1) * LANES], q_heads[hd]) for ji, hd in work]
        sps, lss = [], []
        for z in zs:
            neg_abs = lax.bitcast_convert_type(lax.bitcast_convert_type(z, jnp.uint32) | jnp.uint32(0x80000000), F32)
            ls = jnp.minimum(z, 0.0) - jnp.log2(1.0 + jnp.exp2(neg_abs))
            sp = z - ls
            if diag:
                sp = jnp.where(mask, sp, 0.0)
            sps.append(sp)
            lss.append(ls)
        afters = []
        for sp in sps:
            hi = sp.astype(BF16)
            lo = (sp - hi.astype(F32)).astype(BF16)
            afters.append(_dot(tri, jnp.concatenate([hi, lo], axis=0)))
        cs = [c_ref[hd] for hd in heads]
        ws = []
        for i, (ji, hd) in enumerate(work):
            w = jnp.exp2(lss[i] - afters[i][:tk] - cs[hd])
            if diag:
                w = jnp.where(mask, w, 0.0)
            ws.append(w.astype(BF16))
            cs[hd] = cs[hd] + afters[i][tk:tk + 1]
        for hd in heads:
            c_ref[hd] = cs[hd]
        vts = [vt_ref[j] for j in js]
        for hd in heads:
            rows = slice(hd * HEAD_DIM, (hd + 1) * HEAD_DIM)
            upd = None
            for i, (ji, h2) in enumerate(work):
                if h2 == hd:
                    part = _dot(vts[ji][rows, :], ws[i])
                    upd = part if upd is None else upd + part
            acc_ref[rows, :] += upd

    tiles([qi], True)

    def body(jj, carry):
        j = qi - 1 - 2 * jj
        tiles([j, j - 1], False)
        return carry

    lax.fori_loop(0, qi // 2, body, 0)

    @pl.when(qi % 2 == 1)
    def _():
        tiles([0], False)

    o_ref[...] = acc_ref[...].T.astype(o_ref.dtype)


def _suffix_sum_weights(n):
    tri = lax.broadcasted_iota(jnp.int32, (n, n), 1) > lax.broadcasted_iota(jnp.int32, (n, n), 0)
    tri = jnp.concatenate([tri, jnp.ones((SUBLANES, n), jnp.bool_)], axis=0).astype(BF16)
    return jnp.concatenate([tri, tri], axis=1)


def _sb_call(qt, k, vt):
    bsz, seq, bw = k.shape
    nk, tk = vt.shape[1], vt.shape[3]
    tq = tk
    tri = _suffix_sum_weights(tk)
    return pl.pallas_call(
        _sb_kernel,
        out_shape=jax.ShapeDtypeStruct(k.shape, BF16),
        grid=(bsz, seq // tq),
        in_specs=[
            pl.BlockSpec((None, bw, tq), lambda b, i: (b, 0, i)),
            pl.BlockSpec((None, seq, bw), lambda b, i: (b, 0, 0)),
            pl.BlockSpec((None, nk, bw, tk), lambda b, i: (b, 0, 0, 0)),
            _const_spec(tri.shape),
        ],
        out_specs=pl.BlockSpec((None, tq, bw), lambda b, i: (b, i, 0)),
        scratch_shapes=[pltpu.VMEM((bw, tq), F32), pltpu.VMEM((N_HEADS, 1, tq), F32)],
        compiler_params=_params(("parallel", "parallel")),
        name="stick_breaking",
    )(qt, k, vt, tri)


def _mla_kernel(qt_ref, k_ref, vt_ref, o_ref, acc_ref, m_ref, l_ref):
    qi = pl.program_id(1)
    tk = vt_ref.shape[2]
    tq = qt_ref.shape[1]
    heads = range(N_HEADS)
    acc_ref[...] = jnp.zeros_like(acc_ref)
    m_ref[...] = jnp.full_like(m_ref, NEG_BIG)
    l_ref[...] = jnp.zeros_like(l_ref)

    def tiles(j, nt, diag):
        start = pl.multiple_of(j * tk, tk)
        vt = jnp.concatenate([vt_ref[j + i] for i in range(nt)], axis=1) if nt > 1 else vt_ref[j]
        if diag:
            key = lax.broadcasted_iota(jnp.int32, (tk, tq), 0)
            qry = lax.broadcasted_iota(jnp.int32, (tk, tq), 1)
            mask = key <= qry
        ss = [_dot(k_ref[hd, pl.ds(start, nt * tk), :], qt_ref[hd * LANES:(hd + 1) * LANES, :])
              for hd in heads]
        if diag:
            ss = [jnp.where(mask, s, NEG_BIG) for s in ss]
        ps, alphas = [], []
        for hd in heads:
            m_old = m_ref[hd]
            m_new = jnp.maximum(m_old, jnp.max(ss[hd], axis=0, keepdims=True))
            p = jnp.exp2(ss[hd] - m_new)
            alpha = jnp.exp2(m_old - m_new)
            l_ref[hd] = alpha * l_ref[hd] + jnp.sum(p, axis=0, keepdims=True)
            m_ref[hd] = m_new
            ps.append(p.astype(BF16))
            alphas.append(alpha)
        for hd in heads:
            rows = slice(hd * HEAD_DIM, (hd + 1) * HEAD_DIM)
            acc_ref[rows, :] = alphas[hd] * acc_ref[rows, :] + _dot(vt[rows, :], ps[hd])

    tiles(qi, 1, True)

    def body(jj, carry):
        tiles(2 * jj, 2, False)
        return carry

    lax.fori_loop(0, qi // 2, body, 0)

    @pl.when(qi % 2 == 1)
    def _():
        tiles(qi - 1, 1, False)

    out = jnp.concatenate(
        [acc_ref[hd * HEAD_DIM:(hd + 1) * HEAD_DIM, :] / l_ref[hd] for hd in heads], axis=0)
    o_ref[...] = out.T.astype(o_ref.dtype)


def _mla_call(qt, k, vt):
    bsz, _, seq, _ = k.shape
    nk, bw, tk = vt.shape[1:]
    tq = tk
    return pl.pallas_call(
        _mla_kernel,
        out_shape=jax.ShapeDtypeStruct((bsz, seq, bw), BF16),
        grid=(bsz, seq // tq),
        in_specs=[
            pl.BlockSpec((None, N_HEADS * LANES, tq), lambda b, i: (b, 0, i)),
            pl.BlockSpec((None, N_HEADS, seq, LANES), lambda b, i: (b, 0, 0, 0)),
            pl.BlockSpec((None, nk, bw, tk), lambda b, i: (b, 0, 0, 0)),
        ],
        out_specs=pl.BlockSpec((None, tq, bw), lambda b, i: (b, i, 0)),
        scratch_shapes=[pltpu.VMEM((bw, tq), F32), pltpu.VMEM((N_HEADS, 1, tq), F32),
                        pltpu.VMEM((N_HEADS, 1, tq), F32)],
        compiler_params=_params(("parallel", "parallel")),
        name="latent_attention",
    )(qt, k, vt)


def _scan_kernel(r_ref, e_ref, k_ref, v_ref, a_ref, kk_ref, ka_ref, rk_ref, lg_ref, lb_ref,
                 o_ref, h_ref, vec_ref, cum_ref):
    n = HEAD_DIM

    @pl.when(pl.program_id(0) == 0)
    def _():
        h_ref[...] = jnp.zeros_like(h_ref)

    cum_ref[...] = jnp.zeros_like(cum_ref)

    def bcast_row(ref, c):
        return ref[pl.ds(c, 1), :]

    def step(t, carry):
        kt = k_ref[t]
        at = a_ref[t]
        rt = r_ref[t]
        vt = v_ref[t]
        e_prev = cum_ref[...]
        e_cum = e_prev + e_ref[t]
        cum_ref[...] = e_cum
        g_inv = jnp.exp(e_cum)
        kk = kt * kk_ref[...]
        kk = kk * lax.rsqrt(jnp.maximum(jnp.sum(kk * kk, axis=0, keepdims=True), 1e-24))
        km = kt * (1.0 + (at - 1.0) * ka_ref[...])
        vec_ref[0] = kk * jnp.exp(-e_prev)
        vec_ref[1] = kk * at * g_inv
        vec_ref[2] = km * g_inv
        vec_ref[3] = rt * jnp.exp(-e_cum)
        bonus = jnp.sum(rt * km * rk_ref[...], axis=0, keepdims=True)
        sa_parts = [jnp.zeros_like(vt) for _ in range(SCAN_PARTIALS)]
        for c in range(n):
            sa_parts[c % SCAN_PARTIALS] = sa_parts[c % SCAN_PARTIALS] - h_ref[c] * bcast_row(vec_ref.at[0], c)
        sa = functools.reduce(lambda p, q: p + q, sa_parts)
        y_parts = [jnp.zeros_like(vt) for _ in range(SCAN_PARTIALS)]
        for c in range(n):
            hn = h_ref[c] + sa * bcast_row(vec_ref.at[1], c) + vt * bcast_row(vec_ref.at[2], c)
            h_ref[c] = hn
            y_parts[c % SCAN_PARTIALS] = y_parts[c % SCAN_PARTIALS] + hn * bcast_row(vec_ref.at[3], c)
        y = functools.reduce(lambda p, q: p + q, y_parts)
        mean = jnp.mean(y, axis=(0, 1), keepdims=True)
        yc = y - mean
        var = jnp.mean(yc * yc, axis=(0, 1), keepdims=True)
        yn = yc * lax.rsqrt(var + RWKV_GN_EPS) * lg_ref[...] + lb_ref[...]
        o_ref[t] = yn + bonus * vt
        return carry

    lax.fori_loop(0, r_ref.shape[0], step, 0, unroll=SCAN_UNROLL)
    vec_ref[0] = jnp.exp(-cum_ref[...])
    for c in range(n):
        h_ref[c] = h_ref[c] * bcast_row(vec_ref.at[0], c)


def _to_chains(t):
    bsz, seq, _ = t.shape
    return t.reshape(bsz, seq, N_HEADS, HEAD_DIM).transpose(1, 3, 0, 2).reshape(seq, HEAD_DIM, bsz * N_HEADS)


def _param_chains(p, bsz):
    return jnp.tile(p.reshape(N_HEADS, HEAD_DIM).T, (1, bsz))


def _scan_call(r, w, k, v, a, k_k, k_a, r_k, lnx_g, lnx_b):
    bsz, seq, _ = r.shape
    chains = bsz * N_HEADS
    n = HEAD_DIM
    sub = n // SUBLANES
    r, w, k, v, a = (_to_chains(t) for t in (r, w, k, v, a))
    v = v.reshape(seq, sub, SUBLANES, chains)
    k_k, k_a, r_k, lnx_g, lnx_b = (_param_chains(p.reshape(-1), bsz) for p in (k_k, k_a, r_k, lnx_g, lnx_b))
    lnx_g = lnx_g.reshape(sub, SUBLANES, chains)
    lnx_b = lnx_b.reshape(sub, SUBLANES, chains)
    ts = _tile(seq, SCAN_STEPS)
    tspec = pl.BlockSpec((ts, n, chains), lambda i: (i, 0, 0))
    vspec = pl.BlockSpec((ts, sub, SUBLANES, chains), lambda i: (i, 0, 0, 0))
    pspec = _const_spec((n, chains))
    gspec = _const_spec((sub, SUBLANES, chains))
    out = pl.pallas_call(
        _scan_kernel,
        out_shape=jax.ShapeDtypeStruct((seq, sub, SUBLANES, chains), F32),
        grid=(seq // ts,),
        in_specs=[tspec, tspec, tspec, vspec, tspec, pspec, pspec, pspec, gspec, gspec],
        out_specs=vspec,
        scratch_shapes=[pltpu.VMEM((n, sub, SUBLANES, chains), F32), pltpu.VMEM((4, n, chains), F32),
                        pltpu.VMEM((n, chains), F32)],
        compiler_params=_params(("arbitrary",)),
        name="rwkv7_scan",
    )(r, w, k, v, a, k_k, k_a, r_k, lnx_g, lnx_b)
    out = out.reshape(seq, n, bsz, N_HEADS).transpose(2, 0, 3, 1)
    return out.reshape(bsz, seq, BRANCH_W)


def _merge_body(x, shift, scale, gate_mod, g, wg_ref, ya, gr, yb, yc, bw_ref, wo_ref):
    d = x.shape[1]
    h = _modulated_norm(x, g, shift, scale).astype(BF16)
    ys = ((ya * gr).astype(BF16), yb, yc)
    merged = None
    for n in range(N_BRANCHES):
        gate = jax.nn.sigmoid(_dot(h, wg_ref[:, n * d:(n + 1) * d]))
        term = gate * _dot(ys[n], bw_ref[n])
        merged = term if merged is None else merged + term
    return x + gate_mod * _dot(merged.astype(BF16), wo_ref[...])


def _merge_ffn_kernel(x_ref, sh1_ref, sc1_ref, gt1_ref, sh2_ref, sc2_ref, gt2_ref, g1_ref, g2_ref,
                      wg_ref, ya_ref, gr_ref, yb_ref, yc_ref, bw_ref, wo_ref, wi2_ref, wo2_ref, *rest, final):
    if final:
        fg_ref, o_ref = rest
    else:
        (o_ref,) = rest
    x = _merge_body(x_ref[...], sh1_ref[...], sc1_ref[...], gt1_ref[...], g1_ref[...], wg_ref,
                    ya_ref[...], gr_ref[...], yb_ref[...], yc_ref[...], bw_ref, wo_ref)
    out = _ffn_body(x, sh2_ref[...], sc2_ref[...], gt2_ref[...], g2_ref[...], wi2_ref, wo2_ref)
    if final:
        out = _rms(out, fg_ref[...])
    o_ref[...] = out


def _merge_ffn_call(x, mod5, l, norm_g4, w_gate, ya, gr, yb, yc, branch_w, w_out, w_in2, w_out2, final_g=None):
    bsz, seq, d = x.shape
    bw = BRANCH_W
    tm = _tile(seq, FFN_ROWS)
    final = final_g is not None
    row = lambda b, i: (b, i, 0)
    xspec = pl.BlockSpec((None, tm, d), row)
    bspec = pl.BlockSpec((None, tm, bw), row)
    in_specs = [xspec] + [_mod_spec(l, j, d) for j in range(3, 9)] + [
        _layer_spec(norm_g4.shape, l, 1), _layer_spec(norm_g4.shape, l, 2),
        _layer_spec(w_gate.shape, l), bspec, bspec, bspec, bspec,
        _layer_spec(branch_w.shape, l), _layer_spec(w_out.shape, l),
        _layer_spec(w_in2.shape, l), _layer_spec(w_out2.shape, l),
    ]
    args = [x] + [mod5] * 6 + [norm_g4, norm_g4, w_gate, ya, gr, yb, yc, branch_w, w_out, w_in2, w_out2]
    if final:
        in_specs.append(_const_spec((1, d)))
        args.append(final_g.reshape(1, d))
    return pl.pallas_call(
        functools.partial(_merge_ffn_kernel, final=final),
        out_shape=jax.ShapeDtypeStruct(x.shape, F32),
        grid=(bsz, seq // tm),
        in_specs=in_specs,
        out_specs=xspec,
        compiler_params=_params(("parallel", "parallel")),
        name="merge_ffn",
    )(*args)


def kernel(x, c, positions, ada_w, ada_b, norm_g, ffn1_w_in, ffn1_w_out, mix_w_in, rwkv_mu, rwkv_w0, rwkv_w2, rwkv_a0, rwkv_a2, rwkv_g2, rwkv_k_k, rwkv_k_a, rwkv_r_k, rwkv_lnx_g, rwkv_lnx_b, mla_q_norm_g, mla_w_uq, mla_kv_norm_g, mla_w_ukv, branch_w, mix_w_out, ffn2_w_in, ffn2_w_out, final_norm_g):
    depth, _, d = norm_g.shape
    mod5 = _ada_call(c, ada_w, ada_b)
    cos_t, sin_t = _rope_call(positions)
    norm_g4 = norm_g.reshape(depth, 3, 1, d)
    w1_in, w1_out, w2_in, w2_out, w_mix_out = (
        _cast_call(w) for w in (ffn1_w_in, ffn1_w_out, ffn2_w_in, ffn2_w_out, mix_w_out))
    w_branch = _cast_call(branch_w.reshape(depth * N_BRANCHES, BRANCH_W, d), rows=BRANCH_W).reshape(branch_w.shape)
    pw, w_gate = _prep_weights(mix_w_in, rwkv_mu, rwkv_w0, rwkv_a0, rwkv_w2, rwkv_a2, rwkv_g2,
                               mla_q_norm_g, mla_w_uq, mla_kv_norm_g, mla_w_ukv)
    for l in range(depth):
        x = _ffn_call(x, mod5, l, norm_g4, w1_in, w1_out)
        (r, k, v, w, a, gr, sqt, sk, svt, mqt, mk, mvt) = _prep_call(x, mod5, l, norm_g4, pw, cos_t, sin_t)
        ya = _scan_call(r, w, k, v, a, rwkv_k_k[l], rwkv_k_a[l], rwkv_r_k[l], rwkv_lnx_g[l], rwkv_lnx_b[l])
        yb = _sb_call(sqt, sk, svt)
        yc = _mla_call(mqt, mk, mvt)
        x = _merge_ffn_call(x, mod5, l, norm_g4, w_gate, ya, gr, yb, yc, w_branch, w_mix_out, w2_in, w2_out,
                            final_g=final_norm_g if l == depth - 1 else None)
    return x
```

```python
import functools
import math

import jax
import jax.numpy as jnp
from jax import lax
from jax.experimental import pallas as pl
from jax.experimental.pallas import tpu as pltpu

F32 = jnp.float32
BF16 = jnp.bfloat16

HEAD_DIM = 64
N_HEADS = 6
BRANCH_W = N_HEADS * HEAD_DIM
N_BRANCHES = 3
DECAY_LORA = 64
AAA_LORA = 64
GATE_LORA = 160
RWKV_GN_EPS = 64e-5
MLA_Q_LORA = 256
MLA_KV_LORA = 128
MLA_NOPE_DIM = 64
MLA_ROPE_DIM = 32
MLA_V_DIM = 64
ROPE_THETA = 10000.0
MACARON_WEIGHT = 0.5
NORM_EPS = 1e-6
N_ADA = 9
RWKV_COLS = 3 * BRANCH_W + DECAY_LORA + AAA_LORA + GATE_LORA

LANES = 128
SUBLANES = 8
MXU_WIDTH = 256
VMEM_LIMIT_BYTES = 56 * 1024 * 1024

FFN_ROWS = 512
PREP_ROWS = 512
ATT_TILE = 256
CAST_ROWS = 256
SCAN_STEPS = 64
SCAN_UNROLL = 2
SCAN_PARTIALS = 2

RWKV_PAD = 1536
GATE_LORA_PAD = RWKV_PAD - (3 * BRANCH_W + DECAY_LORA + AAA_LORA)
NEG_BIG = -1e30
LOG2E = math.log2(math.e)


def _tile(n, pref):
    t = min(n, pref)
    assert n % t == 0, (n, t)
    return t


def _params(sem):
    return pltpu.CompilerParams(dimension_semantics=sem, vmem_limit_bytes=VMEM_LIMIT_BYTES)


def _const_spec(shape):
    nd = len(shape)
    return pl.BlockSpec(shape, lambda *_: (0,) * nd)


def _resident_spec(shape):
    nd = len(shape)
    return pl.BlockSpec(shape, lambda *_: (0,) * nd, pipeline_mode=pl.Buffered(1))


def _layer_spec(shape, *lead):
    n_lead = len(lead)
    tail = tuple(shape[n_lead:])
    return pl.BlockSpec((None,) * n_lead + tail, lambda *_: tuple(lead) + (0,) * len(tail),
                        pipeline_mode=pl.Buffered(1))


def _mod_spec(l, j, d):
    return pl.BlockSpec((None, None, None, 1, d), lambda b, *_: (l, j, b, 0, 0))


def _modulated_norm(x, g, shift, scale):
    ms = jnp.mean(x * x, axis=-1, keepdims=True)
    y = x * lax.rsqrt(ms + NORM_EPS) * g
    return y * (1.0 + scale) + shift


def _rms(x, g):
    ms = jnp.mean(x * x, axis=-1, keepdims=True)
    return x * lax.rsqrt(ms + NORM_EPS) * g


def _dot(a, b):
    return jnp.dot(a, b, preferred_element_type=F32)


def _ada_kernel(c_ref, w_ref, b_ref, o_ref):
    c = c_ref[...]
    d = c.shape[1]
    ca = (c * jax.nn.sigmoid(c)).astype(BF16)
    res = _dot(ca, w_ref[...].astype(BF16))
    for j in range(o_ref.shape[0]):
        o_ref[j] = res[:, j * d:(j + 1) * d] + b_ref[j]


def _ada_call(c, ada_w, ada_b):
    depth, d, _ = ada_w.shape
    bsz = c.shape[0]
    per = 3
    ada_b4 = ada_b.reshape(depth, N_ADA, 1, d)
    out = pl.pallas_call(
        _ada_kernel,
        out_shape=jax.ShapeDtypeStruct((depth, N_ADA, bsz, d), F32),
        grid=(depth, N_ADA // per),
        in_specs=[
            pl.BlockSpec((bsz, d), lambda l, j: (0, 0)),
            pl.BlockSpec((None, d, per * d), lambda l, j: (l, 0, j)),
            pl.BlockSpec((None, per, 1, d), lambda l, j: (l, j, 0, 0)),
        ],
        out_specs=pl.BlockSpec((None, per, bsz, d), lambda l, j: (l, j, 0, 0)),
        compiler_params=_params(("parallel", "parallel")),
        name="ada_mod",
    )(c, ada_w, ada_b4)
    return out.reshape(depth, N_ADA, bsz, 1, d)


def _rope_kernel(pos_ref, freq_ref, cos_ref, sin_ref):
    ang = pos_ref[...].astype(F32) * freq_ref[...]
    lane = lax.broadcasted_iota(jnp.int32, ang.shape, 1)
    in_rope = (lane >= MLA_NOPE_DIM) & (lane < MLA_NOPE_DIM + MLA_ROPE_DIM)
    cos_ref[...] = jnp.where(lane < MLA_NOPE_DIM, 1.0, jnp.where(in_rope, jnp.cos(ang), 0.0))
    sin_ref[...] = jnp.where(in_rope, jnp.sin(ang), 0.0)


def _rope_call(positions):
    bsz, seq = positions.shape
    half = MLA_ROPE_DIM // 2
    inv_freq = 1.0 / (ROPE_THETA ** (jnp.arange(0, MLA_ROPE_DIM, 2, dtype=F32) / MLA_ROPE_DIM))
    freq = jnp.zeros((1, LANES), F32)
    freq = freq.at[0, MLA_NOPE_DIM:MLA_NOPE_DIM + half].set(inv_freq)
    freq = freq.at[0, MLA_NOPE_DIM + half:MLA_NOPE_DIM + 2 * half].set(inv_freq)
    ts = _tile(seq, 512)
    spec = pl.BlockSpec((None, ts, LANES), lambda b, i: (b, i, 0))
    return pl.pallas_call(
        _rope_kernel,
        out_shape=(jax.ShapeDtypeStruct((bsz, seq, LANES), F32),) * 2,
        grid=(bsz, seq // ts),
        in_specs=[pl.BlockSpec((None, ts, 1), lambda b, i: (b, i, 0)), _const_spec((1, LANES))],
        out_specs=(spec, spec),
        compiler_params=_params(("parallel", "parallel")),
        name="rope_tables",
    )(positions.reshape(bsz, seq, 1), freq)


def _ffn_body(x, shift, scale, gate_mod, g, wi_ref, wo_ref):
    d_ff = wo_ref.shape[0]
    assert d_ff % MXU_WIDTH == 0
    h = _modulated_norm(x, g, shift, scale).astype(BF16)
    acc = None
    for c0 in range(0, d_ff, MXU_WIDTH):
        gate = _dot(h, wi_ref[:, c0:c0 + MXU_WIDTH])
        up = _dot(h, wi_ref[:, d_ff + c0:d_ff + c0 + MXU_WIDTH])
        act = (gate * jax.nn.sigmoid(gate) * up).astype(BF16)
        part = _dot(act, wo_ref[c0:c0 + MXU_WIDTH, :])
        acc = part if acc is None else acc + part
    return x + (MACARON_WEIGHT * gate_mod) * acc


def _ffn_kernel(x_ref, sh_ref, sc_ref, gt_ref, g_ref, wi_ref, wo_ref, o_ref):
    o_ref[...] = _ffn_body(x_ref[...], sh_ref[...], sc_ref[...], gt_ref[...], g_ref[...], wi_ref, wo_ref)


def _ffn_call(x, mod5, l, norm_g4, w_in, w_out):
    bsz, seq, d = x.shape
    tm = _tile(seq, FFN_ROWS)
    xspec = pl.BlockSpec((None, tm, d), lambda b, i: (b, i, 0))
    return pl.pallas_call(
        _ffn_kernel,
        out_shape=jax.ShapeDtypeStruct((bsz, seq, d), F32),
        grid=(bsz, seq // tm),
        in_specs=[
            xspec, _mod_spec(l, 0, d), _mod_spec(l, 1, d), _mod_spec(l, 2, d),
            _layer_spec(norm_g4.shape, l, 0), _layer_spec(w_in.shape, l), _layer_spec(w_out.shape, l),
        ],
        out_specs=xspec,
        compiler_params=_params(("parallel", "parallel")),
        name="ffn",
    )(x, mod5, mod5, mod5, norm_g4, w_in, w_out)


def _cast_kernel(x_ref, o_ref):
    o_ref[...] = x_ref[...].astype(o_ref.dtype)


def _cast_call(w, rows=CAST_ROWS):
    depth, r, c = w.shape
    tr = _tile(r, rows)
    spec = pl.BlockSpec((None, tr, c), lambda l, i: (l, i, 0))
    return pl.pallas_call(
        _cast_kernel,
        out_shape=jax.ShapeDtypeStruct(w.shape, BF16),
        grid=(depth, r // tr),
        in_specs=[spec],
        out_specs=spec,
        compiler_params=_params(("parallel", "parallel")),
        name="cast_bf16",
    )(w)


def _store_key_tiles(vt_ref, vt):
    tk = vt_ref.shape[2]
    for j in range(vt_ref.shape[0]):
        vt_ref[j] = vt[:, j * tk:(j + 1) * tk]


def _prep_kernel(x_ref, sh_ref, sc_ref, g_ref, wa_ref, wb_ref, mu_ref, w0_ref, a0_ref,
                 w2a2_ref, g2_ref, qg_ref, kvg_ref, wq_ref, wk_ref, wv_ref, cos_ref, sin_ref,
                 r_ref, k_ref, v_ref, w_ref, a_ref, gr_ref,
                 sqt_ref, sk_ref, svt_ref, mqt_ref, mk_ref, mvt_ref,
                 carry_ref):
    i = pl.program_id(1)
    bw = BRANCH_W

    @pl.when(i == 0)
    def _():
        carry_ref[...] = jnp.zeros_like(carry_ref)

    h = _modulated_norm(x_ref[...], g_ref[...], sh_ref[...], sc_ref[...]).astype(BF16)
    tm = h.shape[0]

    p = _dot(h, wa_ref[...])
    row = lax.broadcasted_iota(jnp.int32, p.shape, 0)
    prev = jnp.where(row == 0, carry_ref[...], pltpu.roll(p, 1, 0))
    carry_ref[...] = p[tm - 1:tm, :]
    ps = p + (prev - p) * mu_ref[...]
    r_ref[...] = ps[:, 0:bw]
    k_ref[...] = ps[:, bw:2 * bw]
    v_ref[...] = ps[:, 2 * bw:3 * bw]
    lora = ps[:, 3 * bw:3 * bw + LANES]
    lane = lax.broadcasted_iota(jnp.int32, lora.shape, 1)
    lora = jnp.where(lane < DECAY_LORA, jnp.tanh(lora), lora).astype(BF16)
    wa = _dot(lora, w2a2_ref[...])
    w_log = -jax.nn.softplus(-(w0_ref[...] + wa[:, :bw])) - 0.5
    w_ref[...] = jnp.exp(w_log)
    a_ref[...] = jax.nn.sigmoid(a0_ref[...] + wa[:, bw:])
    dg = jax.nn.sigmoid(ps[:, 3 * bw + LANES:]).astype(BF16)
    gr_ref[...] = _dot(dg, g2_ref[...])

    pb = _dot(h, wb_ref[...])
    sqt_ref[...] = (pb[:, 0:bw] * (HEAD_DIM ** -0.5 * LOG2E)).T.astype(BF16)
    sk_ref[...] = pb[:, bw:2 * bw].astype(BF16)
    _store_key_tiles(svt_ref, pb[:, 2 * bw:3 * bw].T.astype(BF16))
    off = 3 * bw
    cq = _rms(pb[:, off:off + MLA_Q_LORA], qg_ref[...]).astype(BF16)
    off += MLA_Q_LORA
    ckv = _rms(pb[:, off:off + MLA_KV_LORA], kvg_ref[...]).astype(BF16)
    off += MLA_KV_LORA
    cos = cos_ref[...]
    sin = sin_ref[...]
    kr = pb[:, off:off + LANES] * cos + pb[:, off + LANES:off + 2 * LANES] * sin
    qall = _dot(cq, wq_ref[...])
    kall = _dot(ckv, wk_ref[...])
    _store_key_tiles(mvt_ref, _dot(ckv, wv_ref[...]).T.astype(BF16))
    scale = (MLA_NOPE_DIM + MLA_ROPE_DIM) ** -0.5 * LOG2E
    for hd in range(N_HEADS):
        q = qall[:, 2 * hd * LANES:(2 * hd + 1) * LANES] * cos + qall[:, (2 * hd + 1) * LANES:(2 * hd + 2) * LANES] * sin
        mqt_ref[hd * LANES:(hd + 1) * LANES, :] = (q * scale).T.astype(BF16)
        mk_ref[hd] = (kall[:, hd * LANES:(hd + 1) * LANES] + kr).astype(BF16)


def _rotate_half_cols(w):
    half = MLA_ROPE_DIM // 2
    return jnp.concatenate([-w[..., half:], w[..., :half]], axis=-1)


def _prep_weights(mix_w_in, rwkv_mu, rwkv_w0, rwkv_a0, rwkv_w2, rwkv_a2, rwkv_g2,
                  mla_q_norm_g, mla_w_uq, mla_kv_norm_g, mla_w_ukv):
    depth, d, _ = mix_w_in.shape
    bw = BRANCH_W
    wa = jnp.pad(mix_w_in[..., :RWKV_COLS], ((0, 0), (0, 0), (0, RWKV_PAD - RWKV_COLS))).astype(BF16)
    mu = jnp.pad(rwkv_mu, ((0, 0), (0, RWKV_PAD - RWKV_COLS))).reshape(depth, 1, RWKV_PAD)
    o = RWKV_COLS
    qkv_cq_ckv = mix_w_in[..., o:o + 3 * bw + MLA_Q_LORA + MLA_KV_LORA]
    o += 3 * bw + MLA_Q_LORA + MLA_KV_LORA
    w_kr = mix_w_in[..., o:o + MLA_ROPE_DIM]
    o += MLA_ROPE_DIM
    z_lo = jnp.zeros((depth, d, MLA_NOPE_DIM), F32)
    z_hi = jnp.zeros((depth, d, LANES - MLA_NOPE_DIM - MLA_ROPE_DIM), F32)
    wb = jnp.concatenate([qkv_cq_ckv, z_lo, w_kr, z_hi, z_lo, _rotate_half_cols(w_kr), z_hi], axis=-1).astype(BF16)
    w_gate = mix_w_in[..., o:].astype(BF16)
    w2a2 = jnp.zeros((depth, LANES, 2 * bw), F32)
    w2a2 = w2a2.at[:, :DECAY_LORA, :bw].set(rwkv_w2).at[:, DECAY_LORA:, bw:].set(rwkv_a2).astype(BF16)
    g2 = jnp.pad(rwkv_g2, ((0, 0), (0, GATE_LORA_PAD - GATE_LORA), (0, 0))).astype(BF16)
    qd = MLA_NOPE_DIM + MLA_ROPE_DIM
    uq = mla_w_uq.reshape(depth, MLA_Q_LORA, N_HEADS, qd)
    zq = jnp.zeros((depth, MLA_Q_LORA, N_HEADS, LANES - qd), F32)
    zn = jnp.zeros((depth, MLA_Q_LORA, N_HEADS, MLA_NOPE_DIM), F32)
    wq = jnp.concatenate([uq, zq, zn, _rotate_half_cols(uq[..., MLA_NOPE_DIM:]), zq], axis=-1)
    wq = wq.reshape(depth, MLA_Q_LORA, N_HEADS * 2 * LANES).astype(BF16)
    ukv = mla_w_ukv.reshape(depth, MLA_KV_LORA, N_HEADS, MLA_NOPE_DIM + MLA_V_DIM)
    zk = jnp.zeros((depth, MLA_KV_LORA, N_HEADS, LANES - MLA_NOPE_DIM), F32)
    wk = jnp.concatenate([ukv[..., :MLA_NOPE_DIM], zk], axis=-1).reshape(depth, MLA_KV_LORA, N_HEADS * LANES).astype(BF16)
    wv = ukv[..., MLA_NOPE_DIM:].reshape(depth, MLA_KV_LORA, N_HEADS * MLA_V_DIM).astype(BF16)
    vec = lambda p: p.reshape(depth, 1, -1)
    prep = (wa, wb, mu, vec(rwkv_w0), vec(rwkv_a0), w2a2, g2, vec(mla_q_norm_g), vec(mla_kv_norm_g), wq, wk, wv)
    return prep, w_gate


def _prep_call(x, mod5, l, norm_g4, pw, cos_t, sin_t):
    bsz, seq, d = x.shape
    bw = BRANCH_W
    tm = _tile(seq, PREP_ROWS)
    tk = _tile(seq, ATT_TILE)
    assert tm % tk == 0
    row = lambda b, i: (b, i, 0)
    col = lambda b, i: (b, 0, i)
    xspec = pl.BlockSpec((None, tm, d), row)
    bspec = pl.BlockSpec((None, tm, bw), row)
    lspec = pl.BlockSpec((None, tm, LANES), row)
    f32_out = jax.ShapeDtypeStruct((bsz, seq, bw), F32)
    out_shape = (f32_out,) * 6 + (
        jax.ShapeDtypeStruct((bsz, bw, seq), BF16),
        jax.ShapeDtypeStruct((bsz, seq, bw), BF16),
        jax.ShapeDtypeStruct((bsz, seq // tk, bw, tk), BF16),
        jax.ShapeDtypeStruct((bsz, N_HEADS * LANES, seq), BF16),
        jax.ShapeDtypeStruct((bsz, N_HEADS, seq, LANES), BF16),
        jax.ShapeDtypeStruct((bsz, seq // tk, bw, tk), BF16),
    )
    vt_spec = pl.BlockSpec((None, tm // tk, bw, tk), lambda b, i: (b, i, 0, 0))
    out_specs = (bspec,) * 6 + (
        pl.BlockSpec((None, bw, tm), col),
        bspec,
        vt_spec,
        pl.BlockSpec((None, N_HEADS * LANES, tm), col),
        pl.BlockSpec((None, N_HEADS, tm, LANES), lambda b, i: (b, 0, i, 0)),
        vt_spec,
    )
    return pl.pallas_call(
        _prep_kernel,
        out_shape=out_shape,
        grid=(bsz, seq // tm),
        in_specs=[xspec, _mod_spec(l, 3, d), _mod_spec(l, 4, d), _layer_spec(norm_g4.shape, l, 1)]
                 + [_layer_spec(w.shape, l) for w in pw] + [lspec, lspec],
        out_specs=out_specs,
        scratch_shapes=[pltpu.VMEM((1, RWKV_PAD), F32)],
        compiler_params=_params(("parallel", "arbitrary")),
        name="mixer_prep",
    )(x, mod5, mod5, norm_g4, *pw, cos_t, sin_t)


def _sb_kernel(qt_ref, k_ref, vt_ref, tri_ref, o_ref, acc_ref, c_ref):
    qi = pl.program_id(1)
    tk = vt_ref.shape[2]
    tq = qt_ref.shape[1]
    heads = range(N_HEADS)
    frow = lax.broadcasted_iota(jnp.int32, (LANES, tq), 0)
    q_heads = []
    for hd in heads:
        qt = qt_ref[(hd // 2) * LANES:(hd // 2 + 1) * LANES, :]
        keep = frow < HEAD_DIM if hd % 2 == 0 else frow >= HEAD_DIM
        q_heads.append(jnp.where(keep, qt, jnp.zeros_like(qt)))
    acc_ref[...] = jnp.zeros_like(acc_ref)
    c_ref[...] = jnp.zeros_like(c_ref)

    def tiles(js, diag):
        tri = tri_ref[...]
        if diag:
            key = lax.broadcasted_iota(jnp.int32, (tk, tq), 0)
            qry = lax.broadcasted_iota(jnp.int32, (tk, tq), 1)
            mask = key < qry
        work = [(ji, hd) for ji in range(len(js)) for hd in heads]
        masked = [diag and ji == 0 for ji, _ in work]
        ks = [k_ref[pl.ds(pl.multiple_of(j * tk, tk), tk), :] for j in js]
        zs = [_dot(ks[ji][:, (hd // 2) * LANES:(hd // 2 + 1) * LANES], q_heads[hd]) for ji, hd in work]
        sps, lss = [], []
        for i, z in enumerate(zs):
            neg_abs = lax.bitcast_convert_type(lax.bitcast_convert_type(z, jnp.uint32) | jnp.uint32(0x80000000), F32)
            ls = jnp.minimum(z, 0.0) - jnp.log2(1.0 + jnp.exp2(neg_abs))
            sp = z - ls
            if masked[i]:
                sp = jnp.where(mask, sp, 0.0)
            sps.append(sp)
            lss.append(ls)
        afters = []
        for sp in sps:
            hi = sp.astype(BF16)
            lo = (sp - hi.astype(F32)).astype(BF16)
            afters.append(_dot(tri, jnp.concatenate([hi, lo], axis=0)))
        cs = [c_ref[hd] for hd in heads]
        ws = []
        for i, (ji, hd) in enumerate(work):
            w = jnp.exp2(lss[i] - afters[i][:tk] - cs[hd])
            if masked[i]:
                w = jnp.where(mask, w, 0.0)
            ws.append(w.astype(BF16))
            cs[hd] = cs[hd] + afters[i][tk:tk + 1]
        for hd in heads:
            c_ref[hd] = cs[hd]
        vts = [vt_ref[j] for j in js]
        for hd in heads:
            rows = slice(hd * HEAD_DIM, (hd + 1) * HEAD_DIM)
            upd = None
            for i, (ji, h2) in enumerate(work):
                if h2 == hd:
                    part = _dot(vts[ji][rows, :], ws[i])
                    upd = part if upd is None else upd + part
            acc_ref[rows, :] += upd

    @pl.when(qi == 0)
    def _():
        tiles([0], True)

    @pl.when(qi > 0)
    def _():
        tiles([qi, qi - 1], True)

    def body(jj, carry):
        j = qi - 2 - 2 * jj
        tiles([j, j - 1], False)
        return carry

    lax.fori_loop(0, jnp.maximum(qi - 1, 0) // 2, body, 0)

    @pl.when((qi > 0) & (qi % 2 == 0))
    def _():
        tiles([0], False)

    o_ref[...] = acc_ref[...].T.astype(o_ref.dtype)


def _suffix_sum_weights(n):
    tri = lax.broadcasted_iota(jnp.int32, (n, n), 1) > lax.broadcasted_iota(jnp.int32, (n, n), 0)
    tri = jnp.concatenate([tri, jnp.ones((SUBLANES, n), jnp.bool_)], axis=0).astype(BF16)
    return jnp.concatenate([tri, tri], axis=1)


def _sb_call(qt, k, vt):
    bsz, seq, bw = k.shape
    nk, tk = vt.shape[1], vt.shape[3]
    tq = tk
    tri = _suffix_sum_weights(tk)
    return pl.pallas_call(
        _sb_kernel,
        out_shape=jax.ShapeDtypeStruct(k.shape, BF16),
        grid=(bsz, seq // tq),
        in_specs=[
            pl.BlockSpec((None, bw, tq), lambda b, i: (b, 0, i)),
            pl.BlockSpec((None, seq, bw), lambda b, i: (b, 0, 0)),
            pl.BlockSpec((None, nk, bw, tk), lambda b, i: (b, 0, 0, 0)),
            _const_spec(tri.shape),
        ],
        out_specs=pl.BlockSpec((None, tq, bw), lambda b, i: (b, i, 0)),
        scratch_shapes=[pltpu.VMEM((bw, tq), F32), pltpu.VMEM((N_HEADS, 1, tq), F32)],
        compiler_params=_params(("parallel", "parallel")),
        name="stick_breaking",
    )(qt, k, vt, tri)


def _mla_kernel(qt_ref, k_ref, vt_ref, o_ref, acc_ref, m_ref, l_ref):
    qi = pl.program_id(1)
    tk = vt_ref.shape[2]
    tq = qt_ref.shape[1]
    heads = range(N_HEADS)
    acc_ref[...] = jnp.zeros_like(acc_ref)
    m_ref[...] = jnp.full_like(m_ref, NEG_BIG)
    l_ref[...] = jnp.zeros_like(l_ref)

    def tiles(j, nt, diag):
        start = pl.multiple_of(j * tk, tk)
        vt = jnp.concatenate([vt_ref[j + i] for i in range(nt)], axis=1) if nt > 1 else vt_ref[j]
        if diag:
            key = lax.broadcasted_iota(jnp.int32, (nt * tk, tq), 0)
            qry = lax.broadcasted_iota(jnp.int32, (nt * tk, tq), 1) + (nt - 1) * tk
            mask = key <= qry
        ss = [_dot(k_ref[hd, pl.ds(start, nt * tk), :], qt_ref[hd * LANES:(hd + 1) * LANES, :])
              for hd in heads]
        if diag:
            ss = [jnp.where(mask, s, NEG_BIG) for s in ss]
        ps, alphas = [], []
        for hd in heads:
            m_old = m_ref[hd]
            m_new = jnp.maximum(m_old, jnp.max(ss[hd], axis=0, keepdims=True))
            p = jnp.exp2(ss[hd] - m_new)
            alpha = jnp.exp2(m_old - m_new)
            l_ref[hd] = alpha * l_ref[hd] + jnp.sum(p, axis=0, keepdims=True)
            m_ref[hd] = m_new
            ps.append(p.astype(BF16))
            alphas.append(alpha)
        for hd in heads:
            rows = slice(hd * HEAD_DIM, (hd + 1) * HEAD_DIM)
            acc_ref[rows, :] = alphas[hd] * acc_ref[rows, :] + _dot(vt[rows, :], ps[hd])

    @pl.when(qi == 0)
    def _():
        tiles(0, 1, True)

    @pl.when(qi > 0)
    def _():
        tiles(qi - 1, 2, True)

    def body(jj, carry):
        tiles(2 * jj, 2, False)
        return carry

    rest = jnp.maximum(qi - 1, 0)
    lax.fori_loop(0, rest // 2, body, 0)

    @pl.when(rest % 2 == 1)
    def _():
        tiles(rest - 1, 1, False)

    out = jnp.concatenate(
        [acc_ref[hd * HEAD_DIM:(hd + 1) * HEAD_DIM, :] / l_ref[hd] for hd in heads], axis=0)
    o_ref[...] = out.T.astype(o_ref.dtype)


def _mla_call(qt, k, vt):
    bsz, _, seq, _ = k.shape
    nk, bw, tk = vt.shape[1:]
    tq = tk
    return pl.pallas_call(
        _mla_kernel,
        out_shape=jax.ShapeDtypeStruct((bsz, seq, bw), BF16),
        grid=(bsz, seq // tq),
        in_specs=[
            pl.BlockSpec((None, N_HEADS * LANES, tq), lambda b, i: (b, 0, i)),
            pl.BlockSpec((None, N_HEADS, seq, LANES), lambda b, i: (b, 0, 0, 0)),
            pl.BlockSpec((None, nk, bw, tk), lambda b, i: (b, 0, 0, 0)),
        ],
        out_specs=pl.BlockSpec((None, tq, bw), lambda b, i: (b, i, 0)),
        scratch_shapes=[pltpu.VMEM((bw, tq), F32), pltpu.VMEM((N_HEADS, 1, tq), F32),
                        pltpu.VMEM((N_HEADS, 1, tq), F32)],
        compiler_params=_params(("parallel", "parallel")),
        name="latent_attention",
    )(qt, k, vt)


def _scan_kernel(r_ref, e_ref, k_ref, v_ref, a_ref, kk_ref, ka_ref, rk_ref, lg_ref, lb_ref,
                 o_ref, h_ref, vec_ref, cum_ref):
    n = HEAD_DIM

    @pl.when(pl.program_id(0) == 0)
    def _():
        h_ref[...] = jnp.zeros_like(h_ref)

    cum_ref[...] = jnp.zeros_like(cum_ref)

    def bcast_row(ref, c):
        return ref[pl.ds(c, 1), :]

    def step(t, carry):
        kt = k_ref[t]
        at = a_ref[t]
        rt = r_ref[t]
        vt = v_ref[t]
        e_prev = cum_ref[...]
        e_cum = e_prev + e_ref[t]
        cum_ref[...] = e_cum
        g_inv = jnp.exp(e_cum)
        kk = kt * kk_ref[...]
        kk = kk * lax.rsqrt(jnp.maximum(jnp.sum(kk * kk, axis=0, keepdims=True), 1e-24))
        km = kt * (1.0 + (at - 1.0) * ka_ref[...])
        vec_ref[0] = kk * jnp.exp(-e_prev)
        vec_ref[1] = kk * at * g_inv
        vec_ref[2] = km * g_inv
        vec_ref[3] = rt * jnp.exp(-e_cum)
        bonus = jnp.sum(rt * km * rk_ref[...], axis=0, keepdims=True)
        sa_parts = [jnp.zeros_like(vt) for _ in range(SCAN_PARTIALS)]
        for c in range(n):
            sa_parts[c % SCAN_PARTIALS] = sa_parts[c % SCAN_PARTIALS] - h_ref[c] * bcast_row(vec_ref.at[0], c)
        sa = functools.reduce(lambda p, q: p + q, sa_parts)
        y_parts = [jnp.zeros_like(vt) for _ in range(SCAN_PARTIALS)]
        for c in range(n):
            hn = h_ref[c] + sa * bcast_row(vec_ref.at[1], c) + vt * bcast_row(vec_ref.at[2], c)
            h_ref[c] = hn
            y_parts[c % SCAN_PARTIALS] = y_parts[c % SCAN_PARTIALS] + hn * bcast_row(vec_ref.at[3], c)
        y = functools.reduce(lambda p, q: p + q, y_parts)
        mean = jnp.mean(y, axis=(0, 1), keepdims=True)
        yc = y - mean
        var = jnp.mean(yc * yc, axis=(0, 1), keepdims=True)
        yn = yc * lax.rsqrt(var + RWKV_GN_EPS) * lg_ref[...] + lb_ref[...]
        o_ref[t] = yn + bonus * vt
        return carry

    lax.fori_loop(0, r_ref.shape[0], step, 0, unroll=SCAN_UNROLL)
    vec_ref[0] = jnp.exp(-cum_ref[...])
    for c in range(n):
        h_ref[c] = h_ref[c] * bcast_row(vec_ref.at[0], c)


def _to_chains(t):
    bsz, seq, _ = t.shape
    return t.reshape(bsz, seq, N_HEADS, HEAD_DIM).transpose(1, 3, 0, 2).reshape(seq, HEAD_DIM, bsz * N_HEADS)


def _param_chains(p, bsz):
    return jnp.tile(p.reshape(N_HEADS, HEAD_DIM).T, (1, bsz))


def _scan_call(r, w, k, v, a, k_k, k_a, r_k, lnx_g, lnx_b):
    bsz, seq, _ = r.shape
    chains = bsz * N_HEADS
    n = HEAD_DIM
    sub = n // SUBLANES
    r, w, k, v, a = (_to_chains(t) for t in (r, w, k, v, a))
    v = v.reshape(seq, sub, SUBLANES, chains)
    k_k, k_a, r_k, lnx_g, lnx_b = (_param_chains(p.reshape(-1), bsz) for p in (k_k, k_a, r_k, lnx_g, lnx_b))
    lnx_g = lnx_g.reshape(sub, SUBLANES, chains)
    lnx_b = lnx_b.reshape(sub, SUBLANES, chains)
    ts = _tile(seq, SCAN_STEPS)
    tspec = pl.BlockSpec((ts, n, chains), lambda i: (i, 0, 0))
    vspec = pl.BlockSpec((ts, sub, SUBLANES, chains), lambda i: (i, 0, 0, 0))
    pspec = _const_spec((n, chains))
    gspec = _const_spec((sub, SUBLANES, chains))
    out = pl.pallas_call(
        _scan_kernel,
        out_shape=jax.ShapeDtypeStruct((seq, sub, SUBLANES, chains), F32),
        grid=(seq // ts,),
        in_specs=[tspec, tspec, tspec, vspec, tspec, pspec, pspec, pspec, gspec, gspec],
        out_specs=vspec,
        scratch_shapes=[pltpu.VMEM((n, sub, SUBLANES, chains), F32), pltpu.VMEM((4, n, chains), F32),
                        pltpu.VMEM((n, chains), F32)],
        compiler_params=_params(("arbitrary",)),
        name="rwkv7_scan",
    )(r, w, k, v, a, k_k, k_a, r_k, lnx_g, lnx_b)
    out = out.reshape(seq, n, bsz, N_HEADS).transpose(2, 0, 3, 1)
    return out.reshape(bsz, seq, BRANCH_W)


def _merge_body(x, shift, scale, gate_mod, g, wg_ref, ya, gr, yb, yc, bw_ref, wo_ref):
    d = x.shape[1]
    h = _modulated_norm(x, g, shift, scale).astype(BF16)
    ys = ((ya * gr).astype(BF16), yb, yc)
    merged = None
    for n in range(N_BRANCHES):
        gate = jax.nn.sigmoid(_dot(h, wg_ref[:, n * d:(n + 1) * d]))
        term = gate * _dot(ys[n], bw_ref[n])
        merged = term if merged is None else merged + term
    return x + gate_mod * _dot(merged.astype(BF16), wo_ref[...])


def _merge_ffn_kernel(x_ref, sh1_ref, sc1_ref, gt1_ref, sh2_ref, sc2_ref, gt2_ref, g1_ref, g2_ref,
                      wg_ref, ya_ref, gr_ref, yb_ref, yc_ref, bw_ref, wo_ref, wi2_ref, wo2_ref, *rest, final):
    if final:
        fg_ref, o_ref = rest
    else:
        (o_ref,) = rest
    x = _merge_body(x_ref[...], sh1_ref[...], sc1_ref[...], gt1_ref[...], g1_ref[...], wg_ref,
                    ya_ref[...], gr_ref[...], yb_ref[...], yc_ref[...], bw_ref, wo_ref)
    out = _ffn_body(x, sh2_ref[...], sc2_ref[...], gt2_ref[...], g2_ref[...], wi2_ref, wo2_ref)
    if final:
        out = _rms(out, fg_ref[...])
    o_ref[...] = out


def _merge_ffn_call(x, mod5, l, norm_g4, w_gate, ya, gr, yb, yc, branch_w, w_out, w_in2, w_out2, final_g=None):
    bsz, seq, d = x.shape
    bw = BRANCH_W
    tm = _tile(seq, FFN_ROWS)
    final = final_g is not None
    row = lambda b, i: (b, i, 0)
    xspec = pl.BlockSpec((None, tm, d), row)
    bspec = pl.BlockSpec((None, tm, bw), row)
    in_specs = [xspec] + [_mod_spec(l, j, d) for j in range(3, 9)] + [
        _layer_spec(norm_g4.shape, l, 1), _layer_spec(norm_g4.shape, l, 2),
        _layer_spec(w_gate.shape, l), bspec, bspec, bspec, bspec,
        _layer_spec(branch_w.shape, l), _layer_spec(w_out.shape, l),
        _layer_spec(w_in2.shape, l), _layer_spec(w_out2.shape, l),
    ]
    args = [x] + [mod5] * 6 + [norm_g4, norm_g4, w_gate, ya, gr, yb, yc, branch_w, w_out, w_in2, w_out2]
    if final:
        in_specs.append(_const_spec((1, d)))
        args.append(final_g.reshape(1, d))
    return pl.pallas_call(
        functools.partial(_merge_ffn_kernel, final=final),
        out_shape=jax.ShapeDtypeStruct(x.shape, F32),
        grid=(bsz, seq // tm),
        in_specs=in_specs,
        out_specs=xspec,
        compiler_params=_params(("parallel", "parallel")),
        name="merge_ffn",
    )(*args)


def kernel(x, c, positions, ada_w, ada_b, norm_g, ffn1_w_in, ffn1_w_out, mix_w_in, rwkv_mu, rwkv_w0, rwkv_w2, rwkv_a0, rwkv_a2, rwkv_g2, rwkv_k_k, rwkv_k_a, rwkv_r_k, rwkv_lnx_g, rwkv_lnx_b, mla_q_norm_g, mla_w_uq, mla_kv_norm_g, mla_w_ukv, branch_w, mix_w_out, ffn2_w_in, ffn2_w_out, final_norm_g):
    depth, _, d = norm_g.shape
    mod5 = _ada_call(c, ada_w, ada_b)
    cos_t, sin_t = _rope_call(positions)
    norm_g4 = norm_g.reshape(depth, 3, 1, d)
    w1_in, w1_out, w2_in, w2_out, w_mix_out = (
        _cast_call(w) for w in (ffn1_w_in, ffn1_w_out, ffn2_w_in, ffn2_w_out, mix_w_out))
    w_branch = _cast_call(branch_w.reshape(depth * N_BRANCHES, BRANCH_W, d), rows=BRANCH_W).reshape(branch_w.shape)
    pw, w_gate = _prep_weights(mix_w_in, rwkv_mu, rwkv_w0, rwkv_a0, rwkv_w2, rwkv_a2, rwkv_g2,
                               mla_q_norm_g, mla_w_uq, mla_kv_norm_g, mla_w_ukv)
    for l in range(depth):
        x = _ffn_call(x, mod5, l, norm_g4, w1_in, w1_out)
        (r, k, v, w, a, gr, sqt, sk, svt, mqt, mk, mvt) = _prep_call(x, mod5, l, norm_g4, pw, cos_t, sin_t)
        ya = _scan_call(r, w, k, v, a, rwkv_k_k[l], rwkv_k_a[l], rwkv_r_k[l], rwkv_lnx_g[l], rwkv_lnx_b[l])
        yb = _sb_call(sqt, sk, svt)
        yc = _mla_call(mqt, mk, mvt)
        x = _merge_ffn_call(x, mod5, l, norm_g4, w_gate, ya, gr, yb, yc, w_branch, w_mix_out, w2_in, w2_out,
                            final_g=final_norm_g if l == depth - 1 else None)
    return x
```
